```python
import math
import jax, jax.numpy as jnp
from jax import lax
import numpy as np

D_MODEL = 2048
BATCH = 1
SEQ = 8192
DEPTH = 4

N_MIXERS = 2
N_ATTN_LAYERS = (DEPTH + 1) // 2
N_REC_LAYERS = DEPTH // 2
N_HEADS = 16
HEAD_DIM = D_MODEL // N_HEADS
MOBA_BLOCK = 256
MOBA_TOPK = 3
Q_CHUNK = 32
NUM_BUCKETS = 32
MAX_DISTANCE = 4096
D_RNN = D_MODEL
N_GATE_BLOCKS = 16
GATE_BLOCK = D_RNN // N_GATE_BLOCKS
CONV_WIDTH = 4
LRU_C = 8.0
D_FF = 4 * D_MODEL
EPS = 1e-6

kernel_name = "moba_rglru_interleaved_trunk"


def rms_norm(x, g):
    xf = x.astype(jnp.float32)
    y = xf * lax.rsqrt(jnp.mean(xf * xf, axis=-1, keepdims=True) + EPS)
    return (y * g.astype(jnp.float32)).astype(x.dtype)


def t5_bucket(dist):
    n = jnp.maximum(dist, 0)
    max_exact = NUM_BUCKETS // 2
    nf = jnp.maximum(n, max_exact).astype(jnp.float32)
    large = max_exact + (jnp.log(nf / max_exact) / math.log(MAX_DISTANCE / max_exact)
                         * (NUM_BUCKETS - max_exact)).astype(jnp.int32)
    large = jnp.minimum(large, NUM_BUCKETS - 1)
    return jnp.where(n < max_exact, n, large)


def moba_attention(x, w_qkv, w_o, rel_bias):
    B, S, _ = x.shape
    n_blocks = -(-S // MOBA_BLOCK)
    s_pad = n_blocks * MOBA_BLOCK
    qkv = (x @ w_qkv).reshape(B, S, 3, N_HEADS, HEAD_DIM)
    qkv = jnp.pad(qkv, ((0, 0), (0, s_pad - S), (0, 0), (0, 0), (0, 0)))
    q, k, v = qkv[:, :, 0], qkv[:, :, 1], qkv[:, :, 2]
    kb = k.reshape(B, n_blocks, MOBA_BLOCK, N_HEADS, HEAD_DIM).transpose(0, 3, 1, 2, 4)
    vb = v.reshape(B, n_blocks, MOBA_BLOCK, N_HEADS, HEAD_DIM).transpose(0, 3, 1, 2, 4)
    k_mean = jnp.mean(kb, axis=3)
    bias_hb = rel_bias.T.astype(jnp.float32)
    n_sel = min(MOBA_TOPK, n_blocks - 1)
    scale = HEAD_DIM ** -0.5
    b_idx = jnp.arange(B)[:, None, None, None]
    h_idx = jnp.arange(N_HEADS)[None, None, :, None]
    blk_ids = jnp.arange(n_blocks)
    offs = jnp.arange(MOBA_BLOCK)

    def chunk(ci):
        start = ci * Q_CHUNK
        cur = start // MOBA_BLOCK
        qc = lax.dynamic_slice_in_dim(q, start, Q_CHUNK, axis=1)
        q_pos = start + jnp.arange(Q_CHUNK)
        k_own = lax.dynamic_index_in_dim(kb, cur, axis=2, keepdims=False)
        v_own = lax.dynamic_index_in_dim(vb, cur, axis=2, keepdims=False)
        s_own = jnp.einsum('bqhd,bhjd->bqhj', qc, k_own).astype(jnp.float32) * scale
        dist_own = q_pos[:, None] - (cur * MOBA_BLOCK + offs)[None, :]
        s_own = s_own + bias_hb[:, t5_bucket(dist_own)].transpose(1, 0, 2)[None]
        s_own = jnp.where((dist_own >= 0)[None, :, None, :], s_own, -jnp.inf)
        if n_sel == 0:
            p = jax.nn.softmax(s_own, axis=-1)
            return jnp.einsum('bqhj,bhjd->bqhd', p.astype(v.dtype), v_own)
        gate = jnp.einsum('bqhd,bhnd->bqhn', qc, k_mean).astype(jnp.float32)
        gate = jnp.where(blk_ids < cur, gate, -jnp.inf)
        _, sel = lax.top_k(gate, n_sel)
        sel_valid = sel < cur
        k_sel = kb[b_idx, h_idx, sel]
        v_sel = vb[b_idx, h_idx, sel]
        s_sel = jnp.einsum('bqhd,bqhkjd->bqhkj', qc, k_sel).astype(jnp.float32) * scale
        dist_sel = q_pos[None, :, None, None, None] - (sel[..., None] * MOBA_BLOCK + offs)
        s_sel = s_sel + bias_hb[h_idx[..., None], t5_bucket(dist_sel)]
        s_sel = jnp.where(sel_valid[..., None], s_sel, -jnp.inf)
        logits = jnp.concatenate([s_own, s_sel.reshape(B, Q_CHUNK, N_HEADS, n_sel * MOBA_BLOCK)], axis=-1)
        p = jax.nn.softmax(logits, axis=-1).astype(v.dtype)
        p_own = p[..., :MOBA_BLOCK]
        p_sel = p[..., MOBA_BLOCK:].reshape(B, Q_CHUNK, N_HEADS, n_sel, MOBA_BLOCK)
        return (jnp.einsum('bqhj,bhjd->bqhd', p_own, v_own)
                + jnp.einsum('bqhkj,bqhkjd->bqhd', p_sel, v_sel))

    out = lax.map(chunk, jnp.arange(s_pad // Q_CHUNK))
    out = out.transpose(1, 0, 2, 3, 4).reshape(B, s_pad, N_HEADS * HEAD_DIM)[:, :S]
    return out @ w_o


def _lin_rec_combine(c1, c2):
    a1, b1 = c1
    a2, b2 = c2
    return a1 * a2, a2 * b1 + b2


def rglru_block(x, w_in, conv_w, conv_b, w_rg, b_rg, w_ig, b_ig, lam, w_out):
    B, S, _ = x.shape
    xz = x @ w_in
    xr, gate = xz[..., :D_RNN], xz[..., D_RNN:]
    xp = jnp.pad(xr, ((0, 0), (CONV_WIDTH - 1, 0), (0, 0)))
    xc = conv_b + sum(xp[:, j:j + S] * conv_w[j] for j in range(CONV_WIDTH))
    xb = xc.reshape(B, S, N_GATE_BLOCKS, GATE_BLOCK)
    r = jax.nn.sigmoid(jnp.einsum('bsnc,ncd->bsnd', xb, w_rg).reshape(B, S, D_RNN) + b_rg)
    i = jax.nn.sigmoid(jnp.einsum('bsnc,ncd->bsnd', xb, w_ig).reshape(B, S, D_RNN) + b_ig)
    log_a = (LRU_C * r.astype(jnp.float32)) * jax.nn.log_sigmoid(lam.astype(jnp.float32))
    a = jnp.exp(log_a)
    b_in = jnp.sqrt(-jnp.expm1(2.0 * log_a)) * (i * xc).astype(jnp.float32)
    _, h = lax.associative_scan(_lin_rec_combine, (a, b_in), axis=1)
    y = h.astype(x.dtype) * jax.nn.gelu(gate)
    return y @ w_out


def squared_relu_mlp(x, w1, w2):
    return jnp.square(jax.nn.relu(x @ w1)) @ w2


def setup_inputs(seed: int = 0) -> dict:
    key = jax.random.key(seed)
    ks = jax.random.split(key, 20)
    f32 = jnp.float32
    nrm = lambda k, shape, s: jax.random.normal(k, shape, f32) * s
    u = jax.random.uniform(ks[13], (N_REC_LAYERS, D_RNN), f32, 0.9, 0.999)
    a0 = u ** (1.0 / LRU_C)
    lam = jnp.log(a0) - jnp.log1p(-a0)
    return {
        "x": nrm(ks[0], (BATCH, SEQ, D_MODEL), 1.0),
        "rel_bias": nrm(ks[1], (NUM_BUCKETS, N_HEADS), 0.5),
        "mix_norm": 1.0 + nrm(ks[2], (DEPTH, D_MODEL), 0.02),
        "mlp_norm": 1.0 + nrm(ks[3], (DEPTH, D_MODEL), 0.02),
        "final_norm": 1.0 + nrm(ks[4], (D_MODEL,), 0.02),
        "attn_w_qkv": nrm(ks[5], (N_ATTN_LAYERS, D_MODEL, 3 * N_HEADS * HEAD_DIM), D_MODEL ** -0.5),
        "attn_w_o": nrm(ks[6], (N_ATTN_LAYERS, N_HEADS * HEAD_DIM, D_MODEL), D_MODEL ** -0.5),
        "rec_w_in": nrm(ks[7], (N_REC_LAYERS, D_MODEL, 2 * D_RNN), D_MODEL ** -0.5),
        "rec_conv_w": nrm(ks[8], (N_REC_LAYERS, CONV_WIDTH, D_RNN), CONV_WIDTH ** -0.5),
        "rec_conv_b": nrm(ks[9], (N_REC_LAYERS, D_RNN), 0.01),
        "rec_w_rg": nrm(ks[10], (N_REC_LAYERS, N_GATE_BLOCKS, GATE_BLOCK, GATE_BLOCK), GATE_BLOCK ** -0.5),
        "rec_b_rg": nrm(ks[11], (N_REC_LAYERS, D_RNN), 0.01),
        "rec_w_ig": nrm(ks[12], (N_REC_LAYERS, N_GATE_BLOCKS, GATE_BLOCK, GATE_BLOCK), GATE_BLOCK ** -0.5),
        "rec_b_ig": nrm(ks[14], (N_REC_LAYERS, D_RNN), 0.01),
        "rec_lambda": lam,
        "rec_w_out": nrm(ks[15], (N_REC_LAYERS, D_RNN, D_MODEL), D_RNN ** -0.5),
        "mlp_w1": nrm(ks[16], (DEPTH, D_MODEL, D_FF), D_MODEL ** -0.5),
        "mlp_w2": nrm(ks[17], (DEPTH, D_FF, D_MODEL), D_FF ** -0.5),
    }


def reference(x, rel_bias, mix_norm, mlp_norm, final_norm, attn_w_qkv, attn_w_o,
              rec_w_in, rec_conv_w, rec_conv_b, rec_w_rg, rec_b_rg, rec_w_ig, rec_b_ig,
              rec_lambda, rec_w_out, mlp_w1, mlp_w2):
    for layer in range(DEPTH):
        h = rms_norm(x, mix_norm[layer])
        j = layer // N_MIXERS
        if layer % N_MIXERS == 0:
            h = moba_attention(h, attn_w_qkv[j], attn_w_o[j], rel_bias)
        else:
            h = rglru_block(h, rec_w_in[j], rec_conv_w[j], rec_conv_b[j], rec_w_rg[j], rec_b_rg[j],
                            rec_w_ig[j], rec_b_ig[j], rec_lambda[j], rec_w_out[j])
        x = x + h
        x = x + squared_relu_mlp(rms_norm(x, mlp_norm[layer]), mlp_w1[layer], mlp_w2[layer])
    return rms_norm(x, final_norm)
```

```python
import functools
import math

import jax
import jax.numpy as jnp
from jax import lax
from jax.experimental import pallas as pl
from jax.experimental.pallas import tpu as pltpu

HEAD_DIM = 128
MOBA_BLOCK = 256
MOBA_TOPK = 3
NUM_BUCKETS = 32
MAX_DISTANCE = 4096
CONV_WIDTH = 4
LRU_C = 8.0
GATE_BLOCK = 128
EPS = 1e-6

V7X_VMEM_BYTES = 64 * 1024 * 1024
V7X_LANES = 128
V7X_SUBLANES = 8
VMEM_LIMIT_BYTES = V7X_VMEM_BYTES - 8 * 1024 * 1024

F32 = jnp.float32
BF16 = jnp.bfloat16


def _tile(dim, preferred, align):
    if dim <= preferred:
        return dim
    t = (preferred // align) * align
    while t > align and dim % t:
        t -= align
    assert dim % t == 0, (dim, preferred, align)
    return t


def _params(*semantics):
    return pltpu.CompilerParams(dimension_semantics=semantics, vmem_limit_bytes=VMEM_LIMIT_BYTES)


def _rms_rows(x, g):
    y = x * lax.rsqrt(jnp.mean(x * x, axis=-1, keepdims=True) + EPS)
    return y * g


def _norm_kernel(x_ref, g_ref, o_ref):
    o_ref[...] = _rms_rows(x_ref[...], g_ref[...]).astype(o_ref.dtype)


def _rmsnorm(x, g):
    m, d = x.shape
    tm = _tile(m, 512, V7X_SUBLANES)
    return pl.pallas_call(
        _norm_kernel,
        grid=(m // tm,),
        in_specs=[pl.BlockSpec((tm, d), lambda i: (i, 0)), pl.BlockSpec((1, d), lambda i: (0, 0))],
        out_specs=pl.BlockSpec((tm, d), lambda i: (i, 0)),
        out_shape=jax.ShapeDtypeStruct((m, d), BF16),
        compiler_params=_params("arbitrary"),
        name="rmsnorm_in",
    )(x, g.reshape(1, d))


def _proj_kernel(a_ref, w_ref, o_ref, *, epilogue, scaled_tiles, scale):
    acc = jnp.dot(a_ref[...], w_ref[...], preferred_element_type=F32)
    if epilogue == "relu2":
        acc = jnp.square(jnp.maximum(acc, 0.0))
    elif epilogue == "scale_leading":
        acc = acc * jnp.where(pl.program_id(1) < scaled_tiles, scale, 1.0).astype(F32)
    o_ref[...] = acc.astype(o_ref.dtype)


def _proj(a, w, out_dtype, name, epilogue="none", scaled_cols=0, scale=1.0):
    m, k = a.shape
    n = w.shape[1]
    tm = _tile(m, 1024, V7X_SUBLANES)
    tn = _tile(math.gcd(n, scaled_cols) if scaled_cols else n, 1024, V7X_LANES)
    kern = functools.partial(_proj_kernel, epilogue=epilogue, scaled_tiles=scaled_cols // tn, scale=scale)
    return pl.pallas_call(
        kern,
        grid=(m // tm, n // tn),
        in_specs=[pl.BlockSpec((tm, k), lambda i, j: (i, 0)), pl.BlockSpec((k, tn), lambda i, j: (0, j))],
        out_specs=pl.BlockSpec((tm, tn), lambda i, j: (i, j)),
        out_shape=jax.ShapeDtypeStruct((m, n), out_dtype),
        compiler_params=_params("arbitrary", "arbitrary"),
        name=name,
    )(a, w)


def _resid_kernel(a_ref, w_ref, res_ref, g_ref, *refs, nk, final):
    if final:
        out_refs, acc_ref = refs[:1], refs[1] if nk > 1 else None
    else:
        out_refs, acc_ref = refs[:2], refs[2] if nk > 1 else None

    def finish(x):
        y = _rms_rows(x, g_ref[...])
        if final:
            out_refs[0][...] = y
        else:
            out_refs[0][...] = x
            out_refs[1][...] = y.astype(out_refs[1].dtype)

    part = jnp.dot(a_ref[...], w_ref[...], preferred_element_type=F32)
    if nk == 1:
        finish(res_ref[...] + part)
        return
    kk = pl.program_id(1)

    @pl.when(kk == 0)
    def _():
        acc_ref[...] = res_ref[...] + part

    @pl.when(jnp.logical_and(kk > 0, kk < nk - 1))
    def _():
        acc_ref[...] += part

    @pl.when(kk == nk - 1)
    def _():
        finish(acc_ref[...] + part)


def _resid_proj(a, w, res, g, name, final=False):
    m, k = a.shape
    n = w.shape[1]
    tm = _tile(m, 512, V7X_SUBLANES)
    tk = _tile(k, 2048, V7X_LANES)
    nk = k // tk
    kern = functools.partial(_resid_kernel, nk=nk, final=final)
    row_spec = pl.BlockSpec((tm, n), lambda i, kk: (i, 0))
    if final:
        out_shape = (jax.ShapeDtypeStruct((m, n), F32),)
        out_specs = (row_spec,)
    else:
        out_shape = (jax.ShapeDtypeStruct((m, n), F32), jax.ShapeDtypeStruct((m, n), BF16))
        out_specs = (row_spec, row_spec)
    outs = pl.pallas_call(
        kern,
        grid=(m // tm, nk),
        in_specs=[
            pl.BlockSpec((tm, tk), lambda i, kk: (i, kk)),
            pl.BlockSpec((tk, n), lambda i, kk: (kk, 0)),
            row_spec,
            pl.BlockSpec((1, n), lambda i, kk: (0, 0)),
        ],
        out_specs=out_specs,
        out_shape=out_shape,
        scratch_shapes=[pltpu.VMEM((tm, n), F32)] if nk > 1 else [],
        compiler_params=_params("arbitrary", "arbitrary"),
        name=name,
    )(a, w, res, g.reshape(1, n))
    return outs[0] if final else outs


def _attn_kernel(rb_ref, bkt_ref, q_ref, k_ref, v_ref, o_ref,
                 kmean_ref, vt_ref, bias_ref, madd_ref, *, nb, n_sel):
    bs = MOBA_BLOCK
    h = pl.program_id(0)
    i = pl.program_id(1)
    neg_inf = jnp.float32(-jnp.inf)

    @pl.when(i == 0)
    def _per_head_setup():
        bkt = bkt_ref[...]
        bias_row = jnp.zeros(bkt.shape, F32)
        for b in range(NUM_BUCKETS):
            bias_row = jnp.where(bkt == b, rb_ref[b, h], bias_row)
        for d in range(nb):
            seg = jnp.broadcast_to(bias_row[:, d * bs:(d + 2) * bs], (bs, 2 * bs))
            bias_ref[d] = pltpu.roll(seg, 0, 1, stride=1, stride_axis=0)[:, bs:]
        for j in range(nb):
            kj = k_ref[j * bs:(j + 1) * bs, :].astype(F32)
            kmean_ref[j:j + 1, :] = jnp.mean(kj, axis=0, keepdims=True)
            vt_ref[j] = v_ref[j * bs:(j + 1) * bs, :].astype(F32).T.astype(BF16)

    q = q_ref[...]
    nt = (((1,), (1,)), ((), ()))

    gate = lax.dot_general(kmean_ref[...].astype(BF16), q, nt, preferred_element_type=F32)
    blk = lax.broadcasted_iota(jnp.int32, gate.shape, 0)
    gate = jnp.where(blk < i, gate, neg_inf)
    madd = jnp.full(gate.shape, neg_inf, F32)
    for _ in range(n_sel):
        top = jnp.max(gate, axis=0, keepdims=True)
        idx = jnp.min(jnp.where(gate == top, blk, nb), axis=0, keepdims=True)
        hit = blk == idx
        madd = jnp.where(jnp.logical_and(hit, top > neg_inf), 0.0, madd)
        gate = jnp.where(hit, neg_inf, gate)
    madd_ref[...] = madd

    row0 = pl.multiple_of(i * bs, bs)
    s = lax.dot_general(k_ref[pl.ds(row0, bs), :], q, nt, preferred_element_type=F32) + bias_ref[0]
    kpos = lax.broadcasted_iota(jnp.int32, s.shape, 0)
    qpos = lax.broadcasted_iota(jnp.int32, s.shape, 1)
    s = jnp.where(kpos <= qpos, s, neg_inf)
    m0 = jnp.max(s, axis=0, keepdims=True)
    p = jnp.exp(s - m0)
    l0 = jnp.sum(p, axis=0, keepdims=True)
    acc0 = jnp.dot(vt_ref[i], p.astype(BF16), preferred_element_type=F32)

    def past_block(j, carry):
        m, l, acc = carry
        kj = k_ref[pl.ds(pl.multiple_of(j * bs, bs), bs), :]
        s = lax.dot_general(kj, q, nt, preferred_element_type=F32)
        s = s + bias_ref[i - j] + madd_ref[pl.ds(j, 1), :]
        m_new = jnp.maximum(m, jnp.max(s, axis=0, keepdims=True))
        alpha = jnp.exp(m - m_new)
        p = jnp.exp(s - m_new)
        l = alpha * l + jnp.sum(p, axis=0, keepdims=True)
        acc = alpha * acc + jnp.dot(vt_ref[j], p.astype(BF16), preferred_element_type=F32)
        return m_new, l, acc

    _, l, acc = lax.fori_loop(0, i, past_block, (m0, l0, acc0))
    o_ref[...] = (acc / l).T.astype(o_ref.dtype)


def _t5_bucket(dist):
    n = jnp.maximum(dist, 0)
    max_exact = NUM_BUCKETS // 2
    nf = jnp.maximum(n, max_exact).astype(F32)
    large = max_exact + (jnp.log(nf / max_exact) / math.log(MAX_DISTANCE / max_exact)
                         * (NUM_BUCKETS - max_exact)).astype(jnp.int32)
    large = jnp.minimum(large, NUM_BUCKETS - 1)
    return jnp.where(n < max_exact, n, large)


def _moba_attention(qkv, rel_bias):
    s, three_d = qkv.shape
    d = three_d // 3
    nh = d // HEAD_DIM
    bs = MOBA_BLOCK
    assert s % bs == 0 and d % HEAD_DIM == 0
    nb = s // bs
    n_sel = min(MOBA_TOPK, nb - 1)
    bkt = _t5_bucket(jnp.arange(-bs, s, dtype=jnp.int32)).reshape(1, s + bs)
    kern = functools.partial(_attn_kernel, nb=nb, n_sel=n_sel)
    return pl.pallas_call(
        kern,
        grid=(nh, nb),
        in_specs=[
            pl.BlockSpec(memory_space=pltpu.SMEM),
            pl.BlockSpec((1, s + bs), lambda h, i: (0, 0)),
            pl.BlockSpec((bs, HEAD_DIM), lambda h, i: (i, h)),
            pl.BlockSpec((s, HEAD_DIM), lambda h, i: (0, nh + h)),
            pl.BlockSpec((s, HEAD_DIM), lambda h, i: (0, 2 * nh + h)),
        ],
        out_specs=pl.BlockSpec((bs, HEAD_DIM), lambda h, i: (i, h)),
        out_shape=jax.ShapeDtypeStruct((s, d), BF16),
        scratch_shapes=[
            pltpu.VMEM((nb, HEAD_DIM), F32),
            pltpu.VMEM((nb, HEAD_DIM, bs), BF16),
            pltpu.VMEM((nb, bs, bs), F32),
            pltpu.VMEM((nb, bs), F32),
        ],
        compiler_params=_params("arbitrary", "arbitrary"),
        name="moba_attention",
    )(rel_bias, bkt, qkv, qkv, qkv)


def _rglru_kernel(xz_ref, cw_ref, cb_ref, wrg_ref, brg_ref, wig_ref, big_ref, lam_ref, y_ref,
                  xbuf_ref, xc_ref, a_ref, b_ref, h_ref, *, tt, c):
    pad = V7X_SUBLANES

    @pl.when(pl.program_id(0) == 0)
    def _():
        xbuf_ref[0:pad, :] = jnp.zeros((pad, c), F32)
        h_ref[...] = jnp.zeros((1, c), F32)

    xbuf_ref[pad:pad + tt, :] = xz_ref[:, :c]
    xc = cb_ref[...]
    for j in range(CONV_WIDTH):
        lag = CONV_WIDTH - 1 - j
        xc = xc + xbuf_ref[pad - lag:pad - lag + tt, :] * cw_ref[j:j + 1, :]
    xc_ref[...] = xc
    xbuf_ref[0:pad, :] = xbuf_ref[tt:tt + pad, :]

    for n in range(c // GATE_BLOCK):
        sl = slice(n * GATE_BLOCK, (n + 1) * GATE_BLOCK)
        xcn = xc_ref[:, sl]
        xb = xcn.astype(BF16)
        r = jax.nn.sigmoid(jnp.dot(xb, wrg_ref[n], preferred_element_type=F32) + brg_ref[:, sl])
        ig = jax.nn.sigmoid(jnp.dot(xb, wig_ref[n], preferred_element_type=F32) + big_ref[:, sl])
        log_a = (LRU_C * r) * jax.nn.log_sigmoid(lam_ref[:, sl])
        a_ref[:, sl] = jnp.exp(log_a)
        th = jnp.tanh(log_a)
        b_ref[:, sl] = jnp.sqrt(-2.0 * th / (1.0 - th)) * (ig * xcn)

    def step(t, hprev):
        hnew = a_ref[pl.ds(t, 1), :] * hprev + b_ref[pl.ds(t, 1), :]
        b_ref[pl.ds(t, 1), :] = hnew
        return hnew

    h_ref[...] = lax.fori_loop(0, tt, step, h_ref[...], unroll=8)
    y_ref[...] = (b_ref[...] * jax.nn.gelu(xz_ref[:, c:])).astype(y_ref.dtype)


def _rglru(xz, conv_w, conv_b, w_rg, b_rg, w_ig, b_ig, lam):
    s, two_c = xz.shape
    c = two_c // 2
    assert c % GATE_BLOCK == 0
    ngb = c // GATE_BLOCK
    tt = _tile(s, 256, V7X_SUBLANES)
    kern = functools.partial(_rglru_kernel, tt=tt, c=c)
    vec = pl.BlockSpec((1, c), lambda t: (0, 0))
    gate_w = pl.BlockSpec((ngb, GATE_BLOCK, GATE_BLOCK), lambda t: (0, 0, 0))
    return pl.pallas_call(
        kern,
        grid=(s // tt,),
        in_specs=[
            pl.BlockSpec((tt, two_c), lambda t: (t, 0)),
            pl.BlockSpec((CONV_WIDTH, c), lambda t: (0, 0)),
            vec, gate_w, vec, gate_w, vec, vec,
        ],
        out_specs=pl.BlockSpec((tt, c), lambda t: (t, 0)),
        out_shape=jax.ShapeDtypeStruct((s, c), BF16),
        scratch_shapes=[
            pltpu.VMEM((tt + 2 * V7X_SUBLANES, c), F32),
            pltpu.VMEM((tt, c), F32),
            pltpu.VMEM((tt, c), F32),
            pltpu.VMEM((tt, c), F32),
            pltpu.VMEM((1, c), F32),
        ],
        compiler_params=_params("arbitrary"),
        name="rglru_core",
    )(xz, conv_w, conv_b.reshape(1, c), w_rg.astype(BF16), b_rg.reshape(1, c),
      w_ig.astype(BF16), b_ig.reshape(1, c), lam.reshape(1, c))


def kernel(x, rel_bias, mix_norm, mlp_norm, final_norm, attn_w_qkv, attn_w_o, rec_w_in, rec_conv_w,
           rec_conv_b, rec_w_rg, rec_b_rg, rec_w_ig, rec_b_ig, rec_lambda, rec_w_out, mlp_w1, mlp_w2):
    b, s, d = x.shape
    assert b == 1, "the trunk kernels handle one sequence"
    depth = mix_norm.shape[0]
    xr = x.reshape(s, d)
    xn = _rmsnorm(xr, mix_norm[0])
    out = None
    for layer in range(depth):
        j = layer // 2
        if layer % 2 == 0:
            qkv = _proj(xn, attn_w_qkv[j].astype(BF16), BF16, "attn_qkv_proj",
                        epilogue="scale_leading", scaled_cols=d, scale=HEAD_DIM ** -0.5)
            mixed = _moba_attention(qkv, rel_bias)
            w_out = attn_w_o[j]
        else:
            xz = _proj(xn, rec_w_in[j].astype(BF16), F32, "rec_in_proj")
            mixed = _rglru(xz, rec_conv_w[j], rec_conv_b[j], rec_w_rg[j], rec_b_rg[j],
                           rec_w_ig[j], rec_b_ig[j], rec_lambda[j])
            w_out = rec_w_out[j]
        xr, xn = _resid_proj(mixed, w_out.astype(BF16), xr, mlp_norm[layer], "mixer_out_proj")
        hidden = _proj(xn, mlp_w1[layer].astype(BF16), BF16, "mlp_up_proj", epilogue="relu2")
        if layer + 1 < depth:
            xr, xn = _resid_proj(hidden, mlp_w2[layer].astype(BF16), xr, mix_norm[layer + 1], "mlp_down_proj")
        else:
            out = _resid_proj(hidden, mlp_w2[layer].astype(BF16), xr, final_norm, "mlp_down_proj_final",
                              final=True)
    return out.reshape(b, s, d)
```

```python
import functools
import math

import jax
import jax.numpy as jnp
from jax import lax
from jax.experimental import pallas as pl
from jax.experimental.pallas import tpu as pltpu

HEAD_DIM = 128
MOBA_BLOCK = 256
MOBA_TOPK = 3
NUM_BUCKETS = 32
MAX_DISTANCE = 4096
CONV_WIDTH = 4
LRU_C = 8.0
GATE_BLOCK = 128
EPS = 1e-6
MASKED = -1e30
ATTN_QUERY_BLOCKS = 4

V7X_VMEM_BYTES = 64 * 1024 * 1024
V7X_LANES = 128
V7X_SUBLANES = 8
VMEM_LIMIT_BYTES = V7X_VMEM_BYTES - 8 * 1024 * 1024

F32 = jnp.float32
BF16 = jnp.bfloat16


def _tile(dim, preferred, align):
    if dim <= preferred:
        return dim
    t = (preferred // align) * align
    while t > align and dim % t:
        t -= align
    assert dim % t == 0, (dim, preferred, align)
    return t


def _params(*semantics):
    return pltpu.CompilerParams(dimension_semantics=semantics, vmem_limit_bytes=VMEM_LIMIT_BYTES)


def _rms_rows(x, g):
    y = x * lax.rsqrt(jnp.mean(x * x, axis=-1, keepdims=True) + EPS)
    return y * g


def _norm_kernel(x_ref, g_ref, o_ref):
    o_ref[...] = _rms_rows(x_ref[...], g_ref[...]).astype(o_ref.dtype)


def _rmsnorm(x, g):
    m, d = x.shape
    tm = _tile(m, 512, V7X_SUBLANES)
    return pl.pallas_call(
        _norm_kernel,
        grid=(m // tm,),
        in_specs=[pl.BlockSpec((tm, d), lambda i: (i, 0)), pl.BlockSpec((1, d), lambda i: (0, 0))],
        out_specs=pl.BlockSpec((tm, d), lambda i: (i, 0)),
        out_shape=jax.ShapeDtypeStruct((m, d), BF16),
        compiler_params=_params("arbitrary"),
        name="rmsnorm_in",
    )(x, g.reshape(1, d))


def _proj_kernel(a_ref, w_ref, o_ref, *, epilogue, scaled_tiles, scale):
    acc = jnp.dot(a_ref[...], w_ref[...], preferred_element_type=F32)
    if epilogue == "relu2":
        acc = jnp.square(jnp.maximum(acc, 0.0))
    elif epilogue == "scale_leading":
        acc = acc * jnp.where(pl.program_id(1) < scaled_tiles, scale, 1.0).astype(F32)
    o_ref[...] = acc.astype(o_ref.dtype)


def _proj(a, w, out_dtype, name, epilogue="none", scaled_cols=0, scale=1.0):
    m, k = a.shape
    n = w.shape[1]
    tm = _tile(m, 1024, V7X_SUBLANES)
    tn = _tile(math.gcd(n, scaled_cols) if scaled_cols else n, 1024, V7X_LANES)
    kern = functools.partial(_proj_kernel, epilogue=epilogue, scaled_tiles=scaled_cols // tn, scale=scale)
    return pl.pallas_call(
        kern,
        grid=(m // tm, n // tn),
        in_specs=[pl.BlockSpec((tm, k), lambda i, j: (i, 0)), pl.BlockSpec((k, tn), lambda i, j: (0, j))],
        out_specs=pl.BlockSpec((tm, tn), lambda i, j: (i, j)),
        out_shape=jax.ShapeDtypeStruct((m, n), out_dtype),
        compiler_params=_params("arbitrary", "arbitrary"),
        name=name,
    )(a, w)


def _resid_kernel(a_ref, w_ref, res_ref, g_ref, *refs, nk, final):
    if final:
        out_refs, acc_ref = refs[:1], refs[1] if nk > 1 else None
    else:
        out_refs, acc_ref = refs[:2], refs[2] if nk > 1 else None

    def finish(x):
        y = _rms_rows(x, g_ref[...])
        if final:
            out_refs[0][...] = y
        else:
            out_refs[0][...] = x
            out_refs[1][...] = y.astype(out_refs[1].dtype)

    part = jnp.dot(a_ref[...], w_ref[...], preferred_element_type=F32)
    if nk == 1:
        finish(res_ref[...] + part)
        return
    kk = pl.program_id(1)

    @pl.when(kk == 0)
    def _():
        acc_ref[...] = res_ref[...] + part

    @pl.when(jnp.logical_and(kk > 0, kk < nk - 1))
    def _():
        acc_ref[...] += part

    @pl.when(kk == nk - 1)
    def _():
        finish(acc_ref[...] + part)


def _resid_proj(a, w, res, g, name, final=False):
    m, k = a.shape
    n = w.shape[1]
    tm = _tile(m, 512, V7X_SUBLANES)
    tk = _tile(k, 2048, V7X_LANES)
    nk = k // tk
    kern = functools.partial(_resid_kernel, nk=nk, final=final)
    row_spec = pl.BlockSpec((tm, n), lambda i, kk: (i, 0))
    if final:
        out_shape = (jax.ShapeDtypeStruct((m, n), F32),)
        out_specs = (row_spec,)
    else:
        out_shape = (jax.ShapeDtypeStruct((m, n), F32), jax.ShapeDtypeStruct((m, n), BF16))
        out_specs = (row_spec, row_spec)
    outs = pl.pallas_call(
        kern,
        grid=(m // tm, nk),
        in_specs=[
            pl.BlockSpec((tm, tk), lambda i, kk: (i, kk)),
            pl.BlockSpec((tk, n), lambda i, kk: (kk, 0)),
            row_spec,
            pl.BlockSpec((1, n), lambda i, kk: (0, 0)),
        ],
        out_specs=out_specs,
        out_shape=out_shape,
        scratch_shapes=[pltpu.VMEM((tm, n), F32)] if nk > 1 else [],
        compiler_params=_params("arbitrary", "arbitrary"),
        name=name,
    )(a, w, res, g.reshape(1, n))
    return outs[0] if final else outs


def _attn_kernel(rb_ref, bkt_ref, q_ref, k_ref, v_ref, o_ref,
                 kmean_ref, vt_ref, bias_ref, madd_ref, acc_ref, *, nb, qb, n_sel):
    bs = MOBA_BLOCK
    qw = qb * bs
    h = pl.program_id(0)
    i0 = pl.program_id(1) * qb
    neg_inf = jnp.float32(-jnp.inf)

    @pl.when(pl.program_id(1) == 0)
    def _per_head_setup():
        bkt = bkt_ref[...]
        bias_row = jnp.zeros(bkt.shape, F32)
        for b in range(NUM_BUCKETS):
            bias_row = jnp.where(bkt == b, rb_ref[b, h], bias_row)
        for d in range(nb):
            seg = jnp.broadcast_to(bias_row[:, d * bs:(d + 2) * bs], (bs, 2 * bs))
            tile = pltpu.roll(seg, 0, 1, stride=1, stride_axis=0)[:, bs:]
            if d == 0:
                kpos = lax.broadcasted_iota(jnp.int32, tile.shape, 0)
                qpos = lax.broadcasted_iota(jnp.int32, tile.shape, 1)
                tile = jnp.where(kpos <= qpos, tile, MASKED)
            bias_ref[d] = tile
        for j in range(nb):
            kj = k_ref[j * bs:(j + 1) * bs, :].astype(F32)
            kmean_ref[j:j + 1, :] = jnp.mean(kj, axis=0, keepdims=True)
            vt_ref[j] = v_ref[j * bs:(j + 1) * bs, :].astype(F32).T.astype(BF16)

    q = q_ref[...]
    nt = (((1,), (1,)), ((), ()))

    gate = lax.dot_general(kmean_ref[...].astype(BF16), q, nt, preferred_element_type=F32)
    blk = lax.broadcasted_iota(jnp.int32, gate.shape, 0)
    cur = i0 + lax.broadcasted_iota(jnp.int32, gate.shape, 1) // bs
    gate = jnp.where(blk < cur, gate, neg_inf)
    madd = jnp.where(blk == cur, 0.0, MASKED).astype(F32)
    for _ in range(n_sel):
        top = jnp.max(gate, axis=0, keepdims=True)
        idx = jnp.min(jnp.where(gate == top, blk, nb), axis=0, keepdims=True)
        hit = blk == idx
        madd = jnp.where(jnp.logical_and(hit, top > neg_inf), 0.0, madd)
        gate = jnp.where(hit, neg_inf, gate)
    madd_ref[...] = madd

    acc_ref[...] = jnp.zeros(acc_ref.shape, F32)

    def key_block(j, carry):
        m, l = carry
        kj = k_ref[pl.ds(pl.multiple_of(j * bs, bs), bs), :]
        s = lax.dot_general(kj, q, nt, preferred_element_type=F32)
        bias = jnp.concatenate([bias_ref[jnp.maximum(i0 + a - j, 0)] for a in range(qb)], axis=1)
        s = s + bias + madd_ref[pl.ds(j, 1), :]
        m_new = jnp.maximum(m, jnp.max(s, axis=0, keepdims=True))
        alpha = jnp.exp(m - m_new)
        p = jnp.exp(s - m_new)
        l = alpha * l + jnp.sum(p, axis=0, keepdims=True)
        acc_ref[...] = alpha * acc_ref[...] + jnp.dot(vt_ref[j], p.astype(BF16), preferred_element_type=F32)
        return m_new, l

    m0 = jnp.full((1, qw), MASKED, F32)
    _, l = lax.fori_loop(0, i0 + qb, key_block, (m0, jnp.zeros((1, qw), F32)))
    o_ref[...] = (acc_ref[...] / l).T.astype(o_ref.dtype)


def _t5_bucket(dist):
    n = jnp.maximum(dist, 0)
    max_exact = NUM_BUCKETS // 2
    nf = jnp.maximum(n, max_exact).astype(F32)
    large = max_exact + (jnp.log(nf / max_exact) / math.log(MAX_DISTANCE / max_exact)
                         * (NUM_BUCKETS - max_exact)).astype(jnp.int32)
    large = jnp.minimum(large, NUM_BUCKETS - 1)
    return jnp.where(n < max_exact, n, large)


def _moba_attention(qkv, rel_bias):
    s, three_d = qkv.shape
    d = three_d // 3
    nh = d // HEAD_DIM
    bs = MOBA_BLOCK
    assert s % bs == 0 and d % HEAD_DIM == 0
    nb = s // bs
    n_sel = min(MOBA_TOPK, nb - 1)
    qb = _tile(nb, ATTN_QUERY_BLOCKS, 1)
    bkt = _t5_bucket(jnp.arange(-bs, s, dtype=jnp.int32)).reshape(1, s + bs)
    kern = functools.partial(_attn_kernel, nb=nb, qb=qb, n_sel=n_sel)
    return pl.pallas_call(
        kern,
        grid=(nh, nb // qb),
        in_specs=[
            pl.BlockSpec(memory_space=pltpu.SMEM),
            pl.BlockSpec((1, s + bs), lambda h, i: (0, 0)),
            pl.BlockSpec((qb * bs, HEAD_DIM), lambda h, i: (i, h)),
            pl.BlockSpec((s, HEAD_DIM), lambda h, i: (0, nh + h)),
            pl.BlockSpec((s, HEAD_DIM), lambda h, i: (0, 2 * nh + h)),
        ],
        out_specs=pl.BlockSpec((qb * bs, HEAD_DIM), lambda h, i: (i, h)),
        out_shape=jax.ShapeDtypeStruct((s, d), BF16),
        scratch_shapes=[
            pltpu.VMEM((nb, HEAD_DIM), F32),
            pltpu.VMEM((nb, HEAD_DIM, bs), BF16),
            pltpu.VMEM((nb, bs, bs), F32),
            pltpu.VMEM((nb, qb * bs), F32),
            pltpu.VMEM((HEAD_DIM, qb * bs), F32),
        ],
        compiler_params=_params("arbitrary", "arbitrary"),
        name="moba_attention",
    )(rel_bias, bkt, qkv, qkv, qkv)


def _rglru_kernel(xz_ref, cw_ref, cb_ref, wrg_ref, brg_ref, wig_ref, big_ref, lam_ref, y_ref,
                  xbuf_ref, xc_ref, a_ref, b_ref, h_ref, *, tt, c):
    pad = V7X_SUBLANES

    @pl.when(pl.program_id(0) == 0)
    def _():
        xbuf_ref[0:pad, :] = jnp.zeros((pad, c), F32)
        h_ref[...] = jnp.zeros((1, c), F32)

    xbuf_ref[pad:pad + tt, :] = xz_ref[:, :c]
    xc = cb_ref[...]
    for j in range(CONV_WIDTH):
        lag = CONV_WIDTH - 1 - j
        xc = xc + xbuf_ref[pad - lag:pad - lag + tt, :] * cw_ref[j:j + 1, :]
    xc_ref[...] = xc
    xbuf_ref[0:pad, :] = xbuf_ref[tt:tt + pad, :]

    for n in range(c // GATE_BLOCK):
        sl = slice(n * GATE_BLOCK, (n + 1) * GATE_BLOCK)
        xcn = xc_ref[:, sl]
        xb = xcn.astype(BF16)
        r = jax.nn.sigmoid(jnp.dot(xb, wrg_ref[n], preferred_element_type=F32) + brg_ref[:, sl])
        ig = jax.nn.sigmoid(jnp.dot(xb, wig_ref[n], preferred_element_type=F32) + big_ref[:, sl])
        log_a = (LRU_C * r) * jax.nn.log_sigmoid(lam_ref[:, sl])
        a_ref[:, sl] = jnp.exp(log_a)
        th = jnp.tanh(log_a)
        b_ref[:, sl] = jnp.sqrt(-2.0 * th / (1.0 - th)) * (ig * xcn)

    def step(t, hprev):
        hnew = a_ref[pl.ds(t, 1), :] * hprev + b_ref[pl.ds(t, 1), :]
        b_ref[pl.ds(t, 1), :] = hnew
        return hnew

    h_ref[...] = lax.fori_loop(0, tt, step, h_ref[...], unroll=8)
    y_ref[...] = (b_ref[...] * jax.nn.gelu(xz_ref[:, c:])).astype(y_ref.dtype)


def _rglru(xz, conv_w, conv_b, w_rg, b_rg, w_ig, b_ig, lam):
    s, two_c = xz.shape
    c = two_c // 2
    assert c % GATE_BLOCK == 0
    ngb = c // GATE_BLOCK
    tt = _tile(s, 256, V7X_SUBLANES)
    kern = functools.partial(_rglru_kernel, tt=tt, c=c)
    vec = pl.BlockSpec((1, c), lambda t: (0, 0))
    gate_w = pl.BlockSpec((ngb, GATE_BLOCK, GATE_BLOCK), lambda t: (0, 0, 0))
    return pl.pallas_call(
        kern,
        grid=(s // tt,),
        in_specs=[
            pl.BlockSpec((tt, two_c), lambda t: (t, 0)),
            pl.BlockSpec((CONV_WIDTH, c), lambda t: (0, 0)),
            vec, gate_w, vec, gate_w, vec, vec,
        ],
        out_specs=pl.BlockSpec((tt, c), lambda t: (t, 0)),
        out_shape=jax.ShapeDtypeStruct((s, c), BF16),
        scratch_shapes=[
            pltpu.VMEM((tt + 2 * V7X_SUBLANES, c), F32),
            pltpu.VMEM((tt, c), F32),
            pltpu.VMEM((tt, c), F32),
            pltpu.VMEM((tt, c), F32),
            pltpu.VMEM((1, c), F32),
        ],
        compiler_params=_params("arbitrary"),
        name="rglru_core",
    )(xz, conv_w, conv_b.reshape(1, c), w_rg.astype(BF16), b_rg.reshape(1, c),
      w_ig.astype(BF16), b_ig.reshape(1, c), lam.reshape(1, c))


def kernel(x, rel_bias, mix_norm, mlp_norm, final_norm, attn_w_qkv, attn_w_o, rec_w_in, rec_conv_w,
           rec_conv_b, rec_w_rg, rec_b_rg, rec_w_ig, rec_b_ig, rec_lambda, rec_w_out, mlp_w1, mlp_w2):
    b, s, d = x.shape
    assert b == 1, "the trunk kernels handle one sequence"
    depth = mix_norm.shape[0]
    xr = x.reshape(s, d)
    xn = _rmsnorm(xr, mix_norm[0])
    out = None
    for layer in range(depth):
        j = layer // 2
        if layer % 2 == 0:
            qkv = _proj(xn, attn_w_qkv[j].astype(BF16), BF16, "attn_qkv_proj",
                        epilogue="scale_leading", scaled_cols=d, scale=HEAD_DIM ** -0.5)
            mixed = _moba_attention(qkv, rel_bias)
            w_out = attn_w_o[j]
        else:
            xz = _proj(xn, rec_w_in[j].astype(BF16), F32, "rec_in_proj")
            mixed = _rglru(xz, rec_conv_w[j], rec_conv_b[j], rec_w_rg[j], rec_b_rg[j],
                           rec_w_ig[j], rec_b_ig[j], rec_lambda[j])
            w_out = rec_w_out[j]
        xr, xn = _resid_proj(mixed, w_out.astype(BF16), xr, mlp_norm[layer], "mixer_out_proj")
        hidden = _proj(xn, mlp_w1[layer].astype(BF16), BF16, "mlp_up_proj", epilogue="relu2")
        if layer + 1 < depth:
            xr, xn = _resid_proj(hidden, mlp_w2[layer].astype(BF16), xr, mix_norm[layer + 1], "mlp_down_proj")
        else:
            out = _resid_proj(hidden, mlp_w2[layer].astype(BF16), xr, final_norm, "mlp_down_proj_final",
                              final=True)
    return out.reshape(b, s, d)
```

```python
import functools
import math

import jax
import jax.numpy as jnp
from jax import lax
from jax.experimental import pallas as pl
from jax.experimental.pallas import tpu as pltpu

HEAD_DIM = 128
MOBA_BLOCK = 256
MOBA_TOPK = 3
NUM_BUCKETS = 32
MAX_DISTANCE = 4096
CONV_WIDTH = 4
LRU_C = 8.0
GATE_BLOCK = 128
EPS = 1e-6
MASKED = -1e30
LOG2E = math.log2(math.e)
ATTN_QUERY_BLOCKS = 4

V7X_VMEM_BYTES = 64 * 1024 * 1024
V7X_LANES = 128
V7X_SUBLANES = 8
VMEM_LIMIT_BYTES = V7X_VMEM_BYTES - 8 * 1024 * 1024

F32 = jnp.float32
BF16 = jnp.bfloat16


def _tile(dim, preferred, align):
    if dim <= preferred:
        return dim
    t = (preferred // align) * align
    while t > align and dim % t:
        t -= align
    assert dim % t == 0, (dim, preferred, align)
    return t


def _params(*semantics):
    return pltpu.CompilerParams(dimension_semantics=semantics, vmem_limit_bytes=VMEM_LIMIT_BYTES)


def _rms_rows(x, g):
    y = x * lax.rsqrt(jnp.mean(x * x, axis=-1, keepdims=True) + EPS)
    return y * g


def _norm_kernel(x_ref, g_ref, o_ref):
    o_ref[...] = _rms_rows(x_ref[...], g_ref[...]).astype(o_ref.dtype)


def _rmsnorm(x, g):
    m, d = x.shape
    tm = _tile(m, 512, V7X_SUBLANES)
    return pl.pallas_call(
        _norm_kernel,
        grid=(m // tm,),
        in_specs=[pl.BlockSpec((tm, d), lambda i: (i, 0)), pl.BlockSpec((1, d), lambda i: (0, 0))],
        out_specs=pl.BlockSpec((tm, d), lambda i: (i, 0)),
        out_shape=jax.ShapeDtypeStruct((m, d), BF16),
        compiler_params=_params("arbitrary"),
        name="rmsnorm_in",
    )(x, g.reshape(1, d))


def _proj_kernel(a_ref, w_ref, o_ref, *, epilogue, scaled_tiles, scale):
    acc = jnp.dot(a_ref[...], w_ref[...], preferred_element_type=F32)
    if epilogue == "relu2":
        acc = jnp.square(jnp.maximum(acc, 0.0))
    elif epilogue == "scale_leading":
        acc = acc * jnp.where(pl.program_id(1) < scaled_tiles, scale, 1.0).astype(F32)
    o_ref[...] = acc.astype(o_ref.dtype)


def _proj(a, w, out_dtype, name, epilogue="none", scaled_cols=0, scale=1.0):
    m, k = a.shape
    n = w.shape[1]
    tm = _tile(m, 1024, V7X_SUBLANES)
    tn = _tile(math.gcd(n, scaled_cols) if scaled_cols else n, 1024, V7X_LANES)
    kern = functools.partial(_proj_kernel, epilogue=epilogue, scaled_tiles=scaled_cols // tn, scale=scale)
    return pl.pallas_call(
        kern,
        grid=(m // tm, n // tn),
        in_specs=[pl.BlockSpec((tm, k), lambda i, j: (i, 0)), pl.BlockSpec((k, tn), lambda i, j: (0, j))],
        out_specs=pl.BlockSpec((tm, tn), lambda i, j: (i, j)),
        out_shape=jax.ShapeDtypeStruct((m, n), out_dtype),
        compiler_params=_params("arbitrary", "arbitrary"),
        name=name,
    )(a, w)


def _resid_kernel(a_ref, w_ref, res_ref, g_ref, *refs, nk, final):
    if final:
        out_refs, acc_ref = refs[:1], refs[1] if nk > 1 else None
    else:
        out_refs, acc_ref = refs[:2], refs[2] if nk > 1 else None

    def finish(x):
        y = _rms_rows(x, g_ref[...])
        if final:
            out_refs[0][...] = y
        else:
            out_refs[0][...] = x
            out_refs[1][...] = y.astype(out_refs[1].dtype)

    part = jnp.dot(a_ref[...], w_ref[...], preferred_element_type=F32)
    if nk == 1:
        finish(res_ref[...] + part)
        return
    kk = pl.program_id(1)

    @pl.when(kk == 0)
    def _():
        acc_ref[...] = res_ref[...] + part

    @pl.when(jnp.logical_and(kk > 0, kk < nk - 1))
    def _():
        acc_ref[...] += part

    @pl.when(kk == nk - 1)
    def _():
        finish(acc_ref[...] + part)


def _resid_proj(a, w, res, g, name, final=False):
    m, k = a.shape
    n = w.shape[1]
    tm = _tile(m, 512, V7X_SUBLANES)
    tk = _tile(k, 2048, V7X_LANES)
    nk = k // tk
    kern = functools.partial(_resid_kernel, nk=nk, final=final)
    row_spec = pl.BlockSpec((tm, n), lambda i, kk: (i, 0))
    if final:
        out_shape = (jax.ShapeDtypeStruct((m, n), F32),)
        out_specs = (row_spec,)
    else:
        out_shape = (jax.ShapeDtypeStruct((m, n), F32), jax.ShapeDtypeStruct((m, n), BF16))
        out_specs = (row_spec, row_spec)
    outs = pl.pallas_call(
        kern,
        grid=(m // tm, nk),
        in_specs=[
            pl.BlockSpec((tm, tk), lambda i, kk: (i, kk)),
            pl.BlockSpec((tk, n), lambda i, kk: (kk, 0)),
            row_spec,
            pl.BlockSpec((1, n), lambda i, kk: (0, 0)),
        ],
        out_specs=out_specs,
        out_shape=out_shape,
        scratch_shapes=[pltpu.VMEM((tm, n), F32)] if nk > 1 else [],
        compiler_params=_params("arbitrary", "arbitrary"),
        name=name,
    )(a, w, res, g.reshape(1, n))
    return outs[0] if final else outs


def _attn_kernel(rb_ref, bkt_ref, q_ref, k_ref, v_ref, o_ref,
                 kmean_ref, kaug_ref, vt_ref, bias_ref, qaug_ref, s0_ref, s1_ref, p0_ref, p1_ref, acc_ref,
                 *, nb, qb, n_sel):
    bs = MOBA_BLOCK
    hd = HEAD_DIM
    qw = qb * bs
    h = pl.program_id(0)
    i0 = pl.program_id(1) * qb
    neg_inf = jnp.float32(-jnp.inf)

    @pl.when(pl.program_id(1) == 0)
    def _per_head_setup():
        bkt = bkt_ref[...]
        bias_row = jnp.zeros(bkt.shape, F32)
        for b in range(NUM_BUCKETS):
            bias_row = jnp.where(bkt == b, rb_ref[b, h] * LOG2E, bias_row)
        for d in range(nb):
            seg = jnp.broadcast_to(bias_row[:, d * bs:(d + 2) * bs], (bs, 2 * bs))
            tile = pltpu.roll(seg, 0, 1, stride=1, stride_axis=0)[:, bs:]
            if d == 0:
                kpos = lax.broadcasted_iota(jnp.int32, tile.shape, 0)
                qpos = lax.broadcasted_iota(jnp.int32, tile.shape, 1)
                tile = jnp.where(kpos <= qpos, tile, MASKED)
            bias_ref[d] = tile
        lane_blk = lax.broadcasted_iota(jnp.int32, (bs, V7X_LANES), 1)
        for j in range(nb):
            kj = k_ref[j * bs:(j + 1) * bs, :]
            kmean_ref[j:j + 1, :] = jnp.mean(kj.astype(F32), axis=0, keepdims=True)
            kaug_ref[j * bs:(j + 1) * bs, :] = jnp.concatenate(
                [kj, jnp.where(lane_blk == j, 1.0, 0.0).astype(BF16)], axis=1)
            vt_ref[j] = v_ref[j * bs:(j + 1) * bs, :].astype(F32).T.astype(BF16)

    q = q_ref[...]
    nt = (((1,), (1,)), ((), ()))

    gate = lax.dot_general(kmean_ref[...].astype(BF16), q, nt, preferred_element_type=F32)
    blk = lax.broadcasted_iota(jnp.int32, gate.shape, 0)
    cur = i0 + lax.broadcasted_iota(jnp.int32, gate.shape, 1) // bs
    gate = jnp.where(blk < cur, gate, neg_inf)
    madd = jnp.where(blk == cur, 0.0, MASKED).astype(F32)
    for _ in range(n_sel):
        top = jnp.max(gate, axis=0, keepdims=True)
        idx = jnp.min(jnp.where(gate == top, blk, nb), axis=0, keepdims=True)
        hit = blk == idx
        madd = jnp.where(jnp.logical_and(hit, top > neg_inf), 0.0, madd)
        gate = jnp.where(hit, neg_inf, gate)
    madd = jnp.concatenate([madd, jnp.zeros((V7X_LANES - nb, qw), F32)], axis=0)
    qaug_ref[:, :hd] = q
    qaug_ref[:, hd:] = madd.T.astype(BF16)

    def scores(j):
        kj = kaug_ref[pl.ds(pl.multiple_of(j * bs, bs), bs), :]
        s = lax.dot_general(kj, qaug_ref[...], nt, preferred_element_type=F32)
        bias = jnp.concatenate([bias_ref[jnp.maximum(i0 + a - j, 0)] for a in range(qb)], axis=1)
        return s + bias

    last = i0 + qb - 1

    def key_block(j, carry, s_cur_ref, s_next_ref, p_prev_ref, p_cur_ref):
        m, l = carry
        s = s_cur_ref[...]
        m_new = jnp.maximum(m, jnp.max(s, axis=0, keepdims=True))
        alpha = jnp.exp2(m - m_new)
        p = jnp.exp2(s - m_new)
        l = alpha * l + jnp.sum(p, axis=0, keepdims=True)
        p_cur_ref[...] = p.astype(BF16)
        pv_prev = jnp.dot(vt_ref[jnp.maximum(j - 1, 0)], p_prev_ref[...], preferred_element_type=F32)
        acc_ref[...] = alpha * (acc_ref[...] + pv_prev)
        s_next_ref[...] = scores(jnp.minimum(j + 1, last))
        return m_new, l

    def key_block_group(g, carry):
        for u in range(0, qb, 2):
            carry = key_block(qb * g + u, carry, s0_ref, s1_ref, p1_ref, p0_ref)
            carry = key_block(qb * g + u + 1, carry, s1_ref, s0_ref, p0_ref, p1_ref)
        return carry

    s0_ref[...] = scores(0)
    p1_ref[...] = jnp.zeros(p1_ref.shape, BF16)
    acc_ref[...] = jnp.zeros(acc_ref.shape, F32)
    m0 = jnp.full((1, qw), MASKED, F32)
    _, l = lax.fori_loop(0, pl.program_id(1) + 1, key_block_group, (m0, jnp.zeros((1, qw), F32)))
    acc = acc_ref[...] + jnp.dot(vt_ref[last], p1_ref[...], preferred_element_type=F32)
    o_ref[...] = (acc / l).T.astype(o_ref.dtype)


def _t5_bucket(dist):
    n = jnp.maximum(dist, 0)
    max_exact = NUM_BUCKETS // 2
    nf = jnp.maximum(n, max_exact).astype(F32)
    large = max_exact + (jnp.log(nf / max_exact) / math.log(MAX_DISTANCE / max_exact)
                         * (NUM_BUCKETS - max_exact)).astype(jnp.int32)
    large = jnp.minimum(large, NUM_BUCKETS - 1)
    return jnp.where(n < max_exact, n, large)


def _moba_attention(qkv, rel_bias):
    s, three_d = qkv.shape
    d = three_d // 3
    nh = d // HEAD_DIM
    bs = MOBA_BLOCK
    assert s % bs == 0 and d % HEAD_DIM == 0
    nb = s // bs
    n_sel = min(MOBA_TOPK, nb - 1)
    qb = _tile(nb, ATTN_QUERY_BLOCKS, 2)
    assert nb <= V7X_LANES, "the block one-hot code occupies one lane tile"
    bkt = _t5_bucket(jnp.arange(-bs, s, dtype=jnp.int32)).reshape(1, s + bs)
    kern = functools.partial(_attn_kernel, nb=nb, qb=qb, n_sel=n_sel)
    return pl.pallas_call(
        kern,
        grid=(nh, nb // qb),
        in_specs=[
            pl.BlockSpec(memory_space=pltpu.SMEM),
            pl.BlockSpec((1, s + bs), lambda h, i: (0, 0)),
            pl.BlockSpec((qb * bs, HEAD_DIM), lambda h, i: (i, h)),
            pl.BlockSpec((s, HEAD_DIM), lambda h, i: (0, nh + h)),
            pl.BlockSpec((s, HEAD_DIM), lambda h, i: (0, 2 * nh + h)),
        ],
        out_specs=pl.BlockSpec((qb * bs, HEAD_DIM), lambda h, i: (i, h)),
        out_shape=jax.ShapeDtypeStruct((s, d), BF16),
        scratch_shapes=[
            pltpu.VMEM((nb, HEAD_DIM), F32),
            pltpu.VMEM((s, HEAD_DIM + V7X_LANES), BF16),
            pltpu.VMEM((nb, HEAD_DIM, bs), BF16),
            pltpu.VMEM((nb, bs, bs), F32),
            pltpu.VMEM((qb * bs, HEAD_DIM + V7X_LANES), BF16),
            pltpu.VMEM((bs, qb * bs), F32),
            pltpu.VMEM((bs, qb * bs), F32),
            pltpu.VMEM((bs, qb * bs), BF16),
            pltpu.VMEM((bs, qb * bs), BF16),
            pltpu.VMEM((HEAD_DIM, qb * bs), F32),
        ],
        compiler_params=_params("arbitrary", "arbitrary"),
        name="moba_attention",
    )(rel_bias, bkt, qkv, qkv, qkv)


def _rglru_kernel(xz_ref, cw_ref, cb_ref, wrg_ref, brg_ref, wig_ref, big_ref, lam_ref, y_ref,
                  xbuf_ref, xc_ref, a_ref, b_ref, h_ref, *, tt, c):
    pad = V7X_SUBLANES

    @pl.when(pl.program_id(0) == 0)
    def _():
        xbuf_ref[0:pad, :] = jnp.zeros((pad, c), F32)
        h_ref[...] = jnp.zeros((1, c), F32)

    xbuf_ref[pad:pad + tt, :] = xz_ref[:, :c]
    xc = cb_ref[...]
    for j in range(CONV_WIDTH):
        lag = CONV_WIDTH - 1 - j
        xc = xc + xbuf_ref[pad - lag:pad - lag + tt, :] * cw_ref[j:j + 1, :]
    xc_ref[...] = xc
    xbuf_ref[0:pad, :] = xbuf_ref[tt:tt + pad, :]

    for n in range(c // GATE_BLOCK):
        sl = slice(n * GATE_BLOCK, (n + 1) * GATE_BLOCK)
        xcn = xc_ref[:, sl]
        xb = xcn.astype(BF16)
        r = jax.nn.sigmoid(jnp.dot(xb, wrg_ref[n], preferred_element_type=F32) + brg_ref[:, sl])
        ig = jax.nn.sigmoid(jnp.dot(xb, wig_ref[n], preferred_element_type=F32) + big_ref[:, sl])
        log_a = (LRU_C * r) * jax.nn.log_sigmoid(lam_ref[:, sl])
        a_ref[:, sl] = jnp.exp(log_a)
        th = jnp.tanh(log_a)
        b_ref[:, sl] = jnp.sqrt(-2.0 * th / (1.0 - th)) * (ig * xcn)

    def step(t, hprev):
        hnew = a_ref[pl.ds(t, 1), :] * hprev + b_ref[pl.ds(t, 1), :]
        b_ref[pl.ds(t, 1), :] = hnew
        return hnew

    h_ref[...] = lax.fori_loop(0, tt, step, h_ref[...], unroll=8)
    y_ref[...] = (b_ref[...] * jax.nn.gelu(xz_ref[:, c:])).astype(y_ref.dtype)


def _rglru(xz, conv_w, conv_b, w_rg, b_rg, w_ig, b_ig, lam):
    s, two_c = xz.shape
    c = two_c // 2
    assert c % GATE_BLOCK == 0
    ngb = c // GATE_BLOCK
    tt = _tile(s, 256, V7X_SUBLANES)
    kern = functools.partial(_rglru_kernel, tt=tt, c=c)
    vec = pl.BlockSpec((1, c), lambda t: (0, 0))
    gate_w = pl.BlockSpec((ngb, GATE_BLOCK, GATE_BLOCK), lambda t: (0, 0, 0))
    return pl.pallas_call(
        kern,
        grid=(s // tt,),
        in_specs=[
            pl.BlockSpec((tt, two_c), lambda t: (t, 0)),
            pl.BlockSpec((CONV_WIDTH, c), lambda t: (0, 0)),
            vec, gate_w, vec, gate_w, vec, vec,
        ],
        out_specs=pl.BlockSpec((tt, c), lambda t: (t, 0)),
        out_shape=jax.ShapeDtypeStruct((s, c), BF16),
        scratch_shapes=[
            pltpu.VMEM((tt + 2 * V7X_SUBLANES, c), F32),
            pltpu.VMEM((tt, c), F32),
            pltpu.VMEM((tt, c), F32),
            pltpu.VMEM((tt, c), F32),
            pltpu.VMEM((1, c), F32),
        ],
        compiler_params=_params("arbitrary"),
        name="rglru_core",
    )(xz, conv_w, conv_b.reshape(1, c), w_rg.astype(BF16), b_rg.reshape(1, c),
      w_ig.astype(BF16), b_ig.reshape(1, c), lam.reshape(1, c))


def kernel(x, rel_bias, mix_norm, mlp_norm, final_norm, attn_w_qkv, attn_w_o, rec_w_in, rec_conv_w,
           rec_conv_b, rec_w_rg, rec_b_rg, rec_w_ig, rec_b_ig, rec_lambda, rec_w_out, mlp_w1, mlp_w2):
    b, s, d = x.shape
    assert b == 1, "the trunk kernels handle one sequence"
    depth = mix_norm.shape[0]
    xr = x.reshape(s, d)
    xn = _rmsnorm(xr, mix_norm[0])
    out = None
    for layer in range(depth):
        j = layer // 2
        if layer % 2 == 0:
            qkv = _proj(xn, attn_w_qkv[j].astype(BF16), BF16, "attn_qkv_proj",
                        epilogue="scale_leading", scaled_cols=d, scale=HEAD_DIM ** -0.5 * LOG2E)
            mixed = _moba_attention(qkv, rel_bias)
            w_out = attn_w_o[j]
        else:
            xz = _proj(xn, rec_w_in[j].astype(BF16), F32, "rec_in_proj")
            mixed = _rglru(xz, rec_conv_w[j], rec_conv_b[j], rec_w_rg[j], rec_b_rg[j],
                           rec_w_ig[j], rec_b_ig[j], rec_lambda[j])
            w_out = rec_w_out[j]
        xr, xn = _resid_proj(mixed, w_out.astype(BF16), xr, mlp_norm[layer], "mixer_out_proj")
        hidden = _proj(xn, mlp_w1[layer].astype(BF16), BF16, "mlp_up_proj", epilogue="relu2")
        if layer + 1 < depth:
            xr, xn = _resid_proj(hidden, mlp_w2[layer].astype(BF16), xr, mix_norm[layer + 1], "mlp_down_proj")
        else:
            out = _resid_proj(hidden, mlp_w2[layer].astype(BF16), xr, final_norm, "mlp_down_proj_final",
                              final=True)
    return out.reshape(b, s, d)
```

```python
import functools
import math

import jax
import jax.numpy as jnp
from jax import lax
from jax.experimental import pallas as pl
from jax.experimental.pallas import tpu as pltpu

HEAD_DIM = 128
MOBA_BLOCK = 256
MOBA_TOPK = 3
NUM_BUCKETS = 32
MAX_DISTANCE = 4096
CONV_WIDTH = 4
LRU_C = 8.0
GATE_BLOCK = 128
EPS = 1e-6
MASKED = -1e30
LOG2E = math.log2(math.e)
ATTN_QUERY_BLOCKS = 4

V7X_VMEM_BYTES = 64 * 1024 * 1024
V7X_LANES = 128
V7X_SUBLANES = 8
VMEM_LIMIT_BYTES = V7X_VMEM_BYTES - 8 * 1024 * 1024

F32 = jnp.float32
BF16 = jnp.bfloat16


def _tile(dim, preferred, align):
    if dim <= preferred:
        return dim
    t = (preferred // align) * align
    while t > align and dim % t:
        t -= align
    assert dim % t == 0, (dim, preferred, align)
    return t


def _params(*semantics):
    return pltpu.CompilerParams(dimension_semantics=semantics, vmem_limit_bytes=VMEM_LIMIT_BYTES)


def _rms_rows(x, g):
    y = x * lax.rsqrt(jnp.mean(x * x, axis=-1, keepdims=True) + EPS)
    return y * g


def _norm_kernel(x_ref, g_ref, o_ref):
    o_ref[...] = _rms_rows(x_ref[...], g_ref[...]).astype(o_ref.dtype)


def _rmsnorm(x, g):
    m, d = x.shape
    tm = _tile(m, 512, V7X_SUBLANES)
    return pl.pallas_call(
        _norm_kernel,
        grid=(m // tm,),
        in_specs=[pl.BlockSpec((tm, d), lambda i: (i, 0)), pl.BlockSpec((1, d), lambda i: (0, 0))],
        out_specs=pl.BlockSpec((tm, d), lambda i: (i, 0)),
        out_shape=jax.ShapeDtypeStruct((m, d), BF16),
        compiler_params=_params("arbitrary"),
        name="rmsnorm_in",
    )(x, g.reshape(1, d))


def _proj_kernel(a_ref, w_ref, o_ref, wb_ref, *, epilogue, scaled_tiles, scale):
    @pl.when(pl.program_id(1) == 0)
    def _():
        wb_ref[...] = w_ref[...].astype(BF16)

    acc = jnp.dot(a_ref[...], wb_ref[...], preferred_element_type=F32)
    if epilogue == "relu2":
        acc = jnp.square(jnp.maximum(acc, 0.0))
    elif epilogue == "scale_leading":
        acc = acc * jnp.where(pl.program_id(0) < scaled_tiles, scale, 1.0).astype(F32)
    o_ref[...] = acc.astype(o_ref.dtype)


def _proj(a, w, layer, out_dtype, name, epilogue="none", scaled_cols=0, scale=1.0):
    m, k = a.shape
    n = w.shape[2]
    tm = _tile(m, 1024, V7X_SUBLANES)
    tn = _tile(math.gcd(n, scaled_cols) if scaled_cols else n, 1024, V7X_LANES)
    kern = functools.partial(_proj_kernel, epilogue=epilogue, scaled_tiles=scaled_cols // tn, scale=scale)
    return pl.pallas_call(
        kern,
        grid=(n // tn, m // tm),
        in_specs=[pl.BlockSpec((tm, k), lambda j, i: (i, 0)),
                  pl.BlockSpec((None, k, tn), lambda j, i: (layer, 0, j))],
        out_specs=pl.BlockSpec((tm, tn), lambda j, i: (i, j)),
        out_shape=jax.ShapeDtypeStruct((m, n), out_dtype),
        scratch_shapes=[pltpu.VMEM((k, tn), BF16)],
        compiler_params=_params("arbitrary", "arbitrary"),
        name=name,
    )(a, w)


def _resid_kernel(a_ref, w_ref, res_ref, g_ref, *refs, final, cast_weight):
    n_out = 1 if final else 2
    out_refs = refs[:n_out]
    if cast_weight:
        wb_ref = refs[n_out]

        @pl.when(pl.program_id(0) == 0)
        def _():
            wb_ref[...] = w_ref[...].astype(BF16)
    else:
        wb_ref = w_ref

    x = res_ref[...] + jnp.dot(a_ref[...], wb_ref[...], preferred_element_type=F32)
    y = _rms_rows(x, g_ref[...])
    if final:
        out_refs[0][...] = y
    else:
        out_refs[0][...] = x
        out_refs[1][...] = y.astype(out_refs[1].dtype)


def _resid_proj(a, w, layer, res, g, name, final=False):
    m, k = a.shape
    n = w.shape[2]
    cast_weight = w.dtype != BF16
    weight_bytes = k * n * (w.dtype.itemsize + (2 if cast_weight else 0))
    row_bytes = 2 * (2 * k + 4 * n + (4 * n if final else 6 * n)) + 4 * n
    tm = _tile(m, max(V7X_LANES, (VMEM_LIMIT_BYTES - weight_bytes) // row_bytes // V7X_LANES * V7X_LANES),
               V7X_SUBLANES)
    kern = functools.partial(_resid_kernel, final=final, cast_weight=cast_weight)
    row_spec = pl.BlockSpec((tm, n), lambda i: (i, 0))
    if final:
        out_shape = (jax.ShapeDtypeStruct((m, n), F32),)
        out_specs = (row_spec,)
    else:
        out_shape = (jax.ShapeDtypeStruct((m, n), F32), jax.ShapeDtypeStruct((m, n), BF16))
        out_specs = (row_spec, row_spec)
    outs = pl.pallas_call(
        kern,
        grid=(m // tm,),
        in_specs=[
            pl.BlockSpec((tm, k), lambda i: (i, 0)),
            pl.BlockSpec((None, k, n), lambda i: (layer, 0, 0), pipeline_mode=pl.Buffered(1)),
            row_spec,
            pl.BlockSpec((1, n), lambda i: (0, 0)),
        ],
        out_specs=out_specs,
        out_shape=out_shape,
        scratch_shapes=[pltpu.VMEM((k, n), BF16)] if cast_weight else [],
        compiler_params=_params("arbitrary"),
        name=name,
    )(a, w, res, g.reshape(1, n))
    return outs[0] if final else outs


def _attn_kernel(rb_ref, bkt_ref, q_ref, k_ref, v_ref, o_ref,
                 kmean_ref, kaug_ref, vt_ref, bias_ref, qaug_ref, s0_ref, s1_ref, p0_ref, p1_ref, acc_ref,
                 *, nb, qb, n_sel):
    bs = MOBA_BLOCK
    hd = HEAD_DIM
    qw = qb * bs
    h = pl.program_id(0)
    i0 = pl.program_id(1) * qb
    neg_inf = jnp.float32(-jnp.inf)

    @pl.when(pl.program_id(1) == 0)
    def _per_head_setup():
        bkt = bkt_ref[...]
        bias_row = jnp.zeros(bkt.shape, F32)
        for b in range(NUM_BUCKETS):
            bias_row = jnp.where(bkt == b, rb_ref[b, h] * LOG2E, bias_row)
        for d in range(nb):
            seg = jnp.broadcast_to(bias_row[:, d * bs:(d + 2) * bs], (bs, 2 * bs))
            tile = pltpu.roll(seg, 0, 1, stride=1, stride_axis=0)[:, bs:]
            if d == 0:
                kpos = lax.broadcasted_iota(jnp.int32, tile.shape, 0)
                qpos = lax.broadcasted_iota(jnp.int32, tile.shape, 1)
                tile = jnp.where(kpos <= qpos, tile, MASKED)
            bias_ref[d] = tile
        lane_blk = lax.broadcasted_iota(jnp.int32, (bs, V7X_LANES), 1)
        for j in range(nb):
            kj = k_ref[j * bs:(j + 1) * bs, :]
            kmean_ref[j:j + 1, :] = jnp.mean(kj.astype(F32), axis=0, keepdims=True)
            kaug_ref[j * bs:(j + 1) * bs, :] = jnp.concatenate(
                [kj, jnp.where(lane_blk == j, 1.0, 0.0).astype(BF16)], axis=1)
            vt_ref[j] = v_ref[j * bs:(j + 1) * bs, :].astype(F32).T.astype(BF16)

    q = q_ref[...]
    nt = (((1,), (1,)), ((), ()))

    gate = lax.dot_general(kmean_ref[...].astype(BF16), q, nt, preferred_element_type=F32)
    blk = lax.broadcasted_iota(jnp.int32, gate.shape, 0)
    cur = i0 + lax.broadcasted_iota(jnp.int32, gate.shape, 1) // bs
    gate = jnp.where(blk < cur, gate, neg_inf)
    madd = jnp.where(blk == cur, 0.0, MASKED).astype(F32)
    for _ in range(n_sel):
        top = jnp.max(gate, axis=0, keepdims=True)
        idx = jnp.min(jnp.where(gate == top, blk, nb), axis=0, keepdims=True)
        hit = blk == idx
        madd = jnp.where(jnp.logical_and(hit, top > neg_inf), 0.0, madd)
        gate = jnp.where(hit, neg_inf, gate)
    madd = jnp.concatenate([madd, jnp.zeros((V7X_LANES - nb, qw), F32)], axis=0)
    qaug_ref[:, :hd] = q
    qaug_ref[:, hd:] = madd.T.astype(BF16)

    def scores(j):
        kj = kaug_ref[pl.ds(pl.multiple_of(j * bs, bs), bs), :]
        s = lax.dot_general(kj, qaug_ref[...], nt, preferred_element_type=F32)
        bias = jnp.concatenate([bias_ref[jnp.maximum(i0 + a - j, 0)] for a in range(qb)], axis=1)
        return s + bias

    last = i0 + qb - 1

    def key_block(j, carry, s_cur_ref, s_next_ref, p_prev_ref, p_cur_ref):
        m, l = carry
        s = s_cur_ref[...]
        m_new = jnp.maximum(m, jnp.max(s, axis=0, keepdims=True))
        alpha = jnp.exp2(m - m_new)
        p = jnp.exp2(s - m_new)
        l = alpha * l + jnp.sum(p, axis=0, keepdims=True)
        p_cur_ref[...] = p.astype(BF16)
        pv_prev = jnp.dot(vt_ref[jnp.maximum(j - 1, 0)], p_prev_ref[...], preferred_element_type=F32)
        acc_ref[...] = alpha * (acc_ref[...] + pv_prev)
        s_next_ref[...] = scores(jnp.minimum(j + 1, last))
        return m_new, l

    def key_block_group(g, carry):
        for u in range(0, qb, 2):
            carry = key_block(qb * g + u, carry, s0_ref, s1_ref, p1_ref, p0_ref)
            carry = key_block(qb * g + u + 1, carry, s1_ref, s0_ref, p0_ref, p1_ref)
        return carry

    s0_ref[...] = scores(0)
    p1_ref[...] = jnp.zeros(p1_ref.shape, BF16)
    acc_ref[...] = jnp.zeros(acc_ref.shape, F32)
    m0 = jnp.full((1, qw), MASKED, F32)
    _, l = lax.fori_loop(0, pl.program_id(1) + 1, key_block_group, (m0, jnp.zeros((1, qw), F32)))
    acc = acc_ref[...] + jnp.dot(vt_ref[last], p1_ref[...], preferred_element_type=F32)
    o_ref[...] = (acc / l).T.astype(o_ref.dtype)


def _t5_bucket(dist):
    n = jnp.maximum(dist, 0)
    max_exact = NUM_BUCKETS // 2
    nf = jnp.maximum(n, max_exact).astype(F32)
    large = max_exact + (jnp.log(nf / max_exact) / math.log(MAX_DISTANCE / max_exact)
                         * (NUM_BUCKETS - max_exact)).astype(jnp.int32)
    large = jnp.minimum(large, NUM_BUCKETS - 1)
    return jnp.where(n < max_exact, n, large)


def _moba_attention(qkv, rel_bias):
    s, three_d = qkv.shape
    d = three_d // 3
    nh = d // HEAD_DIM
    bs = MOBA_BLOCK
    assert s % bs == 0 and d % HEAD_DIM == 0
    nb = s // bs
    n_sel = min(MOBA_TOPK, nb - 1)
    qb = _tile(nb, ATTN_QUERY_BLOCKS, 2)
    assert nb <= V7X_LANES, "the block one-hot code occupies one lane tile"
    bkt = _t5_bucket(jnp.arange(-bs, s, dtype=jnp.int32)).reshape(1, s + bs)
    kern = functools.partial(_attn_kernel, nb=nb, qb=qb, n_sel=n_sel)
    return pl.pallas_call(
        kern,
        grid=(nh, nb // qb),
        in_specs=[
            pl.BlockSpec(memory_space=pltpu.SMEM),
            pl.BlockSpec((1, s + bs), lambda h, i: (0, 0)),
            pl.BlockSpec((qb * bs, HEAD_DIM), lambda h, i: (i, h)),
            pl.BlockSpec((s, HEAD_DIM), lambda h, i: (0, nh + h)),
            pl.BlockSpec((s, HEAD_DIM), lambda h, i: (0, 2 * nh + h)),
        ],
        out_specs=pl.BlockSpec((qb * bs, HEAD_DIM), lambda h, i: (i, h)),
        out_shape=jax.ShapeDtypeStruct((s, d), BF16),
        scratch_shapes=[
            pltpu.VMEM((nb, HEAD_DIM), F32),
            pltpu.VMEM((s, HEAD_DIM + V7X_LANES), BF16),
            pltpu.VMEM((nb, HEAD_DIM, bs), BF16),
            pltpu.VMEM((nb, bs, bs), F32),
            pltpu.VMEM((qb * bs, HEAD_DIM + V7X_LANES), BF16),
            pltpu.VMEM((bs, qb * bs), F32),
            pltpu.VMEM((bs, qb * bs), F32),
            pltpu.VMEM((bs, qb * bs), BF16),
            pltpu.VMEM((bs, qb * bs), BF16),
            pltpu.VMEM((HEAD_DIM, qb * bs), F32),
        ],
        compiler_params=_params("arbitrary", "arbitrary"),
        name="moba_attention",
    )(rel_bias, bkt, qkv, qkv, qkv)


def _rglru_kernel(xz_ref, cw_ref, cb_ref, wrg_ref, brg_ref, wig_ref, big_ref, lam_ref, y_ref,
                  xbuf_ref, xc_ref, a_ref, b_ref, h_ref, *, tt, c):
    pad = V7X_SUBLANES

    @pl.when(pl.program_id(0) == 0)
    def _():
        xbuf_ref[0:pad, :] = jnp.zeros((pad, c), F32)
        h_ref[...] = jnp.zeros((1, c), F32)

    xbuf_ref[pad:pad + tt, :] = xz_ref[:, :c]
    xc = cb_ref[...]
    for j in range(CONV_WIDTH):
        lag = CONV_WIDTH - 1 - j
        xc = xc + xbuf_ref[pad - lag:pad - lag + tt, :] * cw_ref[j:j + 1, :]
    xc_ref[...] = xc
    xbuf_ref[0:pad, :] = xbuf_ref[tt:tt + pad, :]

    for n in range(c // GATE_BLOCK):
        sl = slice(n * GATE_BLOCK, (n + 1) * GATE_BLOCK)
        xcn = xc_ref[:, sl]
        xb = xcn.astype(BF16)
        r = jax.nn.sigmoid(jnp.dot(xb, wrg_ref[n], preferred_element_type=F32) + brg_ref[:, sl])
        ig = jax.nn.sigmoid(jnp.dot(xb, wig_ref[n], preferred_element_type=F32) + big_ref[:, sl])
        log_a = (LRU_C * r) * jax.nn.log_sigmoid(lam_ref[:, sl])
        a_ref[:, sl] = jnp.exp(log_a)
        th = jnp.tanh(log_a)
        b_ref[:, sl] = jnp.sqrt(-2.0 * th / (1.0 - th)) * (ig * xcn)

    def step(t, hprev):
        hnew = a_ref[pl.ds(t, 1), :] * hprev + b_ref[pl.ds(t, 1), :]
        b_ref[pl.ds(t, 1), :] = hnew
        return hnew

    h_ref[...] = lax.fori_loop(0, tt, step, h_ref[...], unroll=8)
    y_ref[...] = (b_ref[...] * jax.nn.gelu(xz_ref[:, c:])).astype(y_ref.dtype)


def _rglru(xz, conv_w, conv_b, w_rg, b_rg, w_ig, b_ig, lam):
    s, two_c = xz.shape
    c = two_c // 2
    assert c % GATE_BLOCK == 0
    ngb = c // GATE_BLOCK
    tt = _tile(s, 256, V7X_SUBLANES)
    kern = functools.partial(_rglru_kernel, tt=tt, c=c)
    vec = pl.BlockSpec((1, c), lambda t: (0, 0))
    gate_w = pl.BlockSpec((ngb, GATE_BLOCK, GATE_BLOCK), lambda t: (0, 0, 0))
    return pl.pallas_call(
        kern,
        grid=(s // tt,),
        in_specs=[
            pl.BlockSpec((tt, two_c), lambda t: (t, 0)),
            pl.BlockSpec((CONV_WIDTH, c), lambda t: (0, 0)),
            vec, gate_w, vec, gate_w, vec, vec,
        ],
        out_specs=pl.BlockSpec((tt, c), lambda t: (t, 0)),
        out_shape=jax.ShapeDtypeStruct((s, c), BF16),
        scratch_shapes=[
            pltpu.VMEM((tt + 2 * V7X_SUBLANES, c), F32),
            pltpu.VMEM((tt, c), F32),
            pltpu.VMEM((tt, c), F32),
            pltpu.VMEM((tt, c), F32),
            pltpu.VMEM((1, c), F32),
        ],
        compiler_params=_params("arbitrary"),
        name="rglru_core",
    )(xz, conv_w, conv_b.reshape(1, c), w_rg.astype(BF16), b_rg.reshape(1, c),
      w_ig.astype(BF16), b_ig.reshape(1, c), lam.reshape(1, c))


def kernel(x, rel_bias, mix_norm, mlp_norm, final_norm, attn_w_qkv, attn_w_o, rec_w_in, rec_conv_w,
           rec_conv_b, rec_w_rg, rec_b_rg, rec_w_ig, rec_b_ig, rec_lambda, rec_w_out, mlp_w1, mlp_w2):
    b, s, d = x.shape
    assert b == 1, "the trunk kernels handle one sequence"
    depth = mix_norm.shape[0]
    xr = x.reshape(s, d)
    xn = _rmsnorm(xr, mix_norm[0])
    mlp_w2_bf16 = mlp_w2.astype(BF16)
    out = None
    for layer in range(depth):
        j = layer // 2
        if layer % 2 == 0:
            qkv = _proj(xn, attn_w_qkv, j, BF16, "attn_qkv_proj",
                        epilogue="scale_leading", scaled_cols=d, scale=HEAD_DIM ** -0.5 * LOG2E)
            mixed = _moba_attention(qkv, rel_bias)
            w_out = attn_w_o
        else:
            xz = _proj(xn, rec_w_in, j, F32, "rec_in_proj")
            mixed = _rglru(xz, rec_conv_w[j], rec_conv_b[j], rec_w_rg[j], rec_b_rg[j],
                           rec_w_ig[j], rec_b_ig[j], rec_lambda[j])
            w_out = rec_w_out
        xr, xn = _resid_proj(mixed, w_out, j, xr, mlp_norm[layer], "mixer_out_proj")
        hidden = _proj(xn, mlp_w1, layer, BF16, "mlp_up_proj", epilogue="relu2")
        if layer + 1 < depth:
            xr, xn = _resid_proj(hidden, mlp_w2_bf16, layer, xr, mix_norm[layer + 1], "mlp_down_proj")
        else:
            out = _resid_proj(hidden, mlp_w2_bf16, layer, xr, final_norm, "mlp_down_proj_final", final=True)
    return out.reshape(b, s, d)
```

```python
import functools
import math

import jax
import jax.numpy as jnp
from jax import lax
from jax.experimental import pallas as pl
from jax.experimental.pallas import tpu as pltpu

HEAD_DIM = 128
MOBA_BLOCK = 256
MOBA_TOPK = 3
NUM_BUCKETS = 32
MAX_DISTANCE = 4096
CONV_WIDTH = 4
LRU_C = 8.0
GATE_BLOCK = 128
EPS = 1e-6
MASKED = -1e30
LOG2E = math.log2(math.e)
ATTN_QUERY_BLOCKS = 4
ATTN_SUM_ROWS = 16

V7X_VMEM_BYTES = 64 * 1024 * 1024
V7X_LANES = 128
V7X_SUBLANES = 8
VMEM_LIMIT_BYTES = V7X_VMEM_BYTES - 8 * 1024 * 1024

F32 = jnp.float32
BF16 = jnp.bfloat16


def _tile(dim, preferred, align):
    if dim <= preferred:
        return dim
    t = (preferred // align) * align
    while t > align and dim % t:
        t -= align
    assert dim % t == 0, (dim, preferred, align)
    return t


def _params(*semantics):
    return pltpu.CompilerParams(dimension_semantics=semantics, vmem_limit_bytes=VMEM_LIMIT_BYTES)


def _rms_rows(x, g):
    y = x * lax.rsqrt(jnp.mean(x * x, axis=-1, keepdims=True) + EPS)
    return y * g


def _norm_kernel(x_ref, g_ref, o_ref):
    o_ref[...] = _rms_rows(x_ref[...], g_ref[...]).astype(o_ref.dtype)


def _rmsnorm(x, g):
    m, d = x.shape
    tm = _tile(m, 512, V7X_SUBLANES)
    return pl.pallas_call(
        _norm_kernel,
        grid=(m // tm,),
        in_specs=[pl.BlockSpec((tm, d), lambda i: (i, 0)), pl.BlockSpec((1, d), lambda i: (0, 0))],
        out_specs=pl.BlockSpec((tm, d), lambda i: (i, 0)),
        out_shape=jax.ShapeDtypeStruct((m, d), BF16),
        compiler_params=_params("arbitrary"),
        name="rmsnorm_in",
    )(x, g.reshape(1, d))


def _proj_kernel(a_ref, w_ref, o_ref, wb_ref, *, epilogue, scaled_tiles, scale):
    @pl.when(pl.program_id(1) == 0)
    def _():
        wb_ref[...] = w_ref[...].astype(BF16)

    acc = jnp.dot(a_ref[...], wb_ref[...], preferred_element_type=F32)
    if epilogue == "relu2":
        acc = jnp.square(jnp.maximum(acc, 0.0))
    elif epilogue == "scale_leading":
        acc = acc * jnp.where(pl.program_id(0) < scaled_tiles, scale, 1.0).astype(F32)
    o_ref[...] = acc.astype(o_ref.dtype)


def _proj(a, w, layer, out_dtype, name, epilogue="none", scaled_cols=0, scale=1.0):
    m, k = a.shape
    n = w.shape[2]
    tm = _tile(m, 1024, V7X_SUBLANES)
    tn = _tile(math.gcd(n, scaled_cols) if scaled_cols else n, 1024, V7X_LANES)
    kern = functools.partial(_proj_kernel, epilogue=epilogue, scaled_tiles=scaled_cols // tn, scale=scale)
    return pl.pallas_call(
        kern,
        grid=(n // tn, m // tm),
        in_specs=[pl.BlockSpec((tm, k), lambda j, i: (i, 0)),
                  pl.BlockSpec((None, k, tn), lambda j, i: (layer, 0, j))],
        out_specs=pl.BlockSpec((tm, tn), lambda j, i: (i, j)),
        out_shape=jax.ShapeDtypeStruct((m, n), out_dtype),
        scratch_shapes=[pltpu.VMEM((k, tn), BF16)],
        compiler_params=_params("arbitrary", "arbitrary"),
        name=name,
    )(a, w)


def _resid_kernel(a_ref, w_ref, res_ref, g_ref, *refs, final, cast_weight):
    n_out = 1 if final else 2
    out_refs = refs[:n_out]
    if cast_weight:
        wb_ref = refs[n_out]

        @pl.when(pl.program_id(0) == 0)
        def _():
            wb_ref[...] = w_ref[...].astype(BF16)
    else:
        wb_ref = w_ref

    x = res_ref[...] + jnp.dot(a_ref[...], wb_ref[...], preferred_element_type=F32)
    y = _rms_rows(x, g_ref[...])
    if final:
        out_refs[0][...] = y
    else:
        out_refs[0][...] = x
        out_refs[1][...] = y.astype(out_refs[1].dtype)


def _resid_proj(a, w, layer, res, g, name, final=False):
    m, k = a.shape
    n = w.shape[2]
    cast_weight = w.dtype != BF16
    weight_bytes = k * n * (w.dtype.itemsize + (2 if cast_weight else 0))
    row_bytes = 2 * (2 * k + 4 * n + (4 * n if final else 6 * n)) + 4 * n
    tm = _tile(m, max(V7X_LANES, (VMEM_LIMIT_BYTES - weight_bytes) // row_bytes // V7X_LANES * V7X_LANES),
               V7X_SUBLANES)
    kern = functools.partial(_resid_kernel, final=final, cast_weight=cast_weight)
    row_spec = pl.BlockSpec((tm, n), lambda i: (i, 0))
    if final:
        out_shape = (jax.ShapeDtypeStruct((m, n), F32),)
        out_specs = (row_spec,)
    else:
        out_shape = (jax.ShapeDtypeStruct((m, n), F32), jax.ShapeDtypeStruct((m, n), BF16))
        out_specs = (row_spec, row_spec)
    outs = pl.pallas_call(
        kern,
        grid=(m // tm,),
        in_specs=[
            pl.BlockSpec((tm, k), lambda i: (i, 0)),
            pl.BlockSpec((None, k, n), lambda i: (layer, 0, 0), pipeline_mode=pl.Buffered(1)),
            row_spec,
            pl.BlockSpec((1, n), lambda i: (0, 0)),
        ],
        out_specs=out_specs,
        out_shape=out_shape,
        scratch_shapes=[pltpu.VMEM((k, n), BF16)] if cast_weight else [],
        compiler_params=_params("arbitrary"),
        name=name,
    )(a, w, res, g.reshape(1, n))
    return outs[0] if final else outs


def _attn_kernel(rb_ref, bkt_ref, q_ref, k_ref, v_ref, o_ref,
                 kmean_ref, kaug_ref, vt_ref, bias_ref, qaug_ref, s0_ref, s1_ref, p0_ref, p1_ref, acc_ref,
                 *, nb, qb, n_sel):
    bs = MOBA_BLOCK
    hd = HEAD_DIM
    qw = qb * bs
    h = pl.program_id(0)
    i0 = pl.program_id(1) * qb
    neg_inf = jnp.float32(-jnp.inf)

    @pl.when(pl.program_id(1) == 0)
    def _per_head_setup():
        bkt = bkt_ref[...]
        bias_row = jnp.zeros(bkt.shape, F32)
        for b in range(NUM_BUCKETS):
            bias_row = jnp.where(bkt == b, rb_ref[b, h] * LOG2E, bias_row)
        for d in range(nb):
            seg = jnp.broadcast_to(bias_row[:, d * bs:(d + 2) * bs], (bs, 2 * bs))
            tile = pltpu.roll(seg, 0, 1, stride=1, stride_axis=0)[:, bs:]
            if d == 0:
                kpos = lax.broadcasted_iota(jnp.int32, tile.shape, 0)
                qpos = lax.broadcasted_iota(jnp.int32, tile.shape, 1)
                tile = jnp.where(kpos <= qpos, tile, MASKED)
            bias_ref[d] = tile
        lane_blk = lax.broadcasted_iota(jnp.int32, (bs, V7X_LANES), 1)
        for j in range(nb):
            kj = k_ref[j * bs:(j + 1) * bs, :]
            kmean_ref[j:j + 1, :] = jnp.mean(kj.astype(F32), axis=0, keepdims=True)
            kaug_ref[j * bs:(j + 1) * bs, :] = jnp.concatenate(
                [kj, jnp.where(lane_blk == j, 1.0, 0.0).astype(BF16)], axis=1)
            ones_row = jnp.where(lax.broadcasted_iota(jnp.int32, (ATTN_SUM_ROWS, bs), 0) == 0, 1.0, 0.0)
            vt_ref[j] = jnp.concatenate([v_ref[j * bs:(j + 1) * bs, :].astype(F32).T, ones_row], axis=0).astype(BF16)

    q = q_ref[...]
    nt = (((1,), (1,)), ((), ()))

    gate = lax.dot_general(kmean_ref[...].astype(BF16), q, nt, preferred_element_type=F32)
    blk = lax.broadcasted_iota(jnp.int32, gate.shape, 0)
    cur = i0 + lax.broadcasted_iota(jnp.int32, gate.shape, 1) // bs
    gate = jnp.where(blk < cur, gate, neg_inf)
    madd = jnp.where(blk == cur, 0.0, MASKED).astype(F32)
    for _ in range(n_sel):
        top = jnp.max(gate, axis=0, keepdims=True)
        idx = jnp.min(jnp.where(gate == top, blk, nb), axis=0, keepdims=True)
        hit = blk == idx
        madd = jnp.where(jnp.logical_and(hit, top > neg_inf), 0.0, madd)
        gate = jnp.where(hit, neg_inf, gate)
    madd = jnp.concatenate([madd, jnp.zeros((V7X_LANES - nb, qw), F32)], axis=0)
    qaug_ref[:, :hd] = q
    qaug_ref[:, hd:] = madd.T.astype(BF16)

    def scores(j):
        kj = kaug_ref[pl.ds(pl.multiple_of(j * bs, bs), bs), :]
        s = lax.dot_general(kj, qaug_ref[...], nt, preferred_element_type=F32)
        bias = jnp.concatenate([bias_ref[jnp.maximum(i0 + a - j, 0)] for a in range(qb)], axis=1)
        return s + bias

    last = i0 + qb - 1

    def key_block(j, m, s_cur_ref, s_next_ref, p_prev_ref, p_cur_ref):
        pv_prev = jnp.dot(vt_ref[jnp.maximum(j - 1, 0)], p_prev_ref[...], preferred_element_type=F32)
        s = s_cur_ref[...]
        m_new = jnp.maximum(m, jnp.max(s, axis=0, keepdims=True))
        p_cur_ref[...] = jnp.exp2(s - m_new).astype(BF16)
        acc_ref[...] = jnp.exp2(m - m_new) * (acc_ref[...] + pv_prev)
        s_next_ref[...] = scores(jnp.minimum(j + 1, last))
        return m_new

    def key_block_group(g, m):
        for u in range(0, qb, 2):
            m = key_block(qb * g + u, m, s0_ref, s1_ref, p1_ref, p0_ref)
            m = key_block(qb * g + u + 1, m, s1_ref, s0_ref, p0_ref, p1_ref)
        return m

    s0_ref[...] = scores(0)
    p1_ref[...] = jnp.zeros(p1_ref.shape, BF16)
    acc_ref[...] = jnp.zeros(acc_ref.shape, F32)
    lax.fori_loop(0, pl.program_id(1) + 1, key_block_group, jnp.full((1, qw), MASKED, F32))
    acc = acc_ref[...] + jnp.dot(vt_ref[last], p1_ref[...], preferred_element_type=F32)
    o_ref[...] = (acc[:hd] / acc[hd:hd + 1]).T.astype(o_ref.dtype)


def _t5_bucket(dist):
    n = jnp.maximum(dist, 0)
    max_exact = NUM_BUCKETS // 2
    nf = jnp.maximum(n, max_exact).astype(F32)
    large = max_exact + (jnp.log(nf / max_exact) / math.log(MAX_DISTANCE / max_exact)
                         * (NUM_BUCKETS - max_exact)).astype(jnp.int32)
    large = jnp.minimum(large, NUM_BUCKETS - 1)
    return jnp.where(n < max_exact, n, large)


def _moba_attention(qkv, rel_bias):
    s, three_d = qkv.shape
    d = three_d // 3
    nh = d // HEAD_DIM
    bs = MOBA_BLOCK
    assert s % bs == 0 and d % HEAD_DIM == 0
    nb = s // bs
    n_sel = min(MOBA_TOPK, nb - 1)
    qb = _tile(nb, ATTN_QUERY_BLOCKS, 2)
    assert nb <= V7X_LANES, "the block one-hot code occupies one lane tile"
    bkt = _t5_bucket(jnp.arange(-bs, s, dtype=jnp.int32)).reshape(1, s + bs)
    kern = functools.partial(_attn_kernel, nb=nb, qb=qb, n_sel=n_sel)
    return pl.pallas_call(
        kern,
        grid=(nh, nb // qb),
        in_specs=[
            pl.BlockSpec(memory_space=pltpu.SMEM),
            pl.BlockSpec((1, s + bs), lambda h, i: (0, 0)),
            pl.BlockSpec((qb * bs, HEAD_DIM), lambda h, i: (i, h)),
            pl.BlockSpec((s, HEAD_DIM), lambda h, i: (0, nh + h)),
            pl.BlockSpec((s, HEAD_DIM), lambda h, i: (0, 2 * nh + h)),
        ],
        out_specs=pl.BlockSpec((qb * bs, HEAD_DIM), lambda h, i: (i, h)),
        out_shape=jax.ShapeDtypeStruct((s, d), BF16),
        scratch_shapes=[
            pltpu.VMEM((nb, HEAD_DIM), F32),
            pltpu.VMEM((s, HEAD_DIM + V7X_LANES), BF16),
            pltpu.VMEM((nb, HEAD_DIM + ATTN_SUM_ROWS, bs), BF16),
            pltpu.VMEM((nb, bs, bs), F32),
            pltpu.VMEM((qb * bs, HEAD_DIM + V7X_LANES), BF16),
            pltpu.VMEM((bs, qb * bs), F32),
            pltpu.VMEM((bs, qb * bs), F32),
            pltpu.VMEM((bs, qb * bs), BF16),
            pltpu.VMEM((bs, qb * bs), BF16),
            pltpu.VMEM((HEAD_DIM + ATTN_SUM_ROWS, qb * bs), F32),
        ],
        compiler_params=_params("arbitrary", "arbitrary"),
        name="moba_attention",
    )(rel_bias, bkt, qkv, qkv, qkv)


def _sigmoid(x):
    return 0.5 * jnp.tanh(0.5 * x) + 0.5


def _rglru_kernel(xz_ref, cw_ref, cb_ref, wrg_ref, brg_ref, wig_ref, big_ref, lam_ref, y_ref,
                  xbuf_ref, xc_ref, a_ref, b_ref, h_ref, *, tt, c):
    pad = V7X_SUBLANES

    @pl.when(pl.program_id(0) == 0)
    def _():
        xbuf_ref[0:pad, :] = jnp.zeros((pad, c), F32)
        h_ref[...] = jnp.zeros((1, c), F32)

    xbuf_ref[pad:pad + tt, :] = xz_ref[:, :c]
    xc = cb_ref[...]
    for j in range(CONV_WIDTH):
        lag = CONV_WIDTH - 1 - j
        xc = xc + xbuf_ref[pad - lag:pad - lag + tt, :] * cw_ref[j:j + 1, :]
    xc_ref[...] = xc
    xbuf_ref[0:pad, :] = xbuf_ref[tt:tt + pad, :]

    for n in range(c // GATE_BLOCK):
        sl = slice(n * GATE_BLOCK, (n + 1) * GATE_BLOCK)
        xcn = xc_ref[:, sl]
        xb = xcn.astype(BF16)
        r = _sigmoid(jnp.dot(xb, wrg_ref[n], preferred_element_type=F32) + brg_ref[:, sl])
        ig = _sigmoid(jnp.dot(xb, wig_ref[n], preferred_element_type=F32) + big_ref[:, sl])
        log_a = (LRU_C * r) * jax.nn.log_sigmoid(lam_ref[:, sl])
        a = jnp.exp(log_a)
        a_ref[:, sl] = a
        w = -jnp.tanh(log_a) * (1.0 + a * a)
        b_ref[:, sl] = jnp.where(w > 0.0, w * lax.rsqrt(w), 0.0) * (ig * xcn)

    def step(t, hprev):
        hnew = a_ref[pl.ds(t, 1), :] * hprev + b_ref[pl.ds(t, 1), :]
        b_ref[pl.ds(t, 1), :] = hnew
        return hnew

    h_ref[...] = lax.fori_loop(0, tt, step, h_ref[...], unroll=8)
    y_ref[...] = (b_ref[...] * jax.nn.gelu(xz_ref[:, c:])).astype(y_ref.dtype)


def _rglru(xz, conv_w, conv_b, w_rg, b_rg, w_ig, b_ig, lam):
    s, two_c = xz.shape
    c = two_c // 2
    assert c % GATE_BLOCK == 0
    ngb = c // GATE_BLOCK
    tt = _tile(s, 256, V7X_SUBLANES)
    kern = functools.partial(_rglru_kernel, tt=tt, c=c)
    vec = pl.BlockSpec((1, c), lambda t: (0, 0))
    gate_w = pl.BlockSpec((ngb, GATE_BLOCK, GATE_BLOCK), lambda t: (0, 0, 0))
    return pl.pallas_call(
        kern,
        grid=(s // tt,),
        in_specs=[
            pl.BlockSpec((tt, two_c), lambda t: (t, 0)),
            pl.BlockSpec((CONV_WIDTH, c), lambda t: (0, 0)),
            vec, gate_w, vec, gate_w, vec, vec,
        ],
        out_specs=pl.BlockSpec((tt, c), lambda t: (t, 0)),
        out_shape=jax.ShapeDtypeStruct((s, c), BF16),
        scratch_shapes=[
            pltpu.VMEM((tt + 2 * V7X_SUBLANES, c), F32),
            pltpu.VMEM((tt, c), F32),
            pltpu.VMEM((tt, c), F32),
            pltpu.VMEM((tt, c), F32),
            pltpu.VMEM((1, c), F32),
        ],
        compiler_params=_params("arbitrary"),
        name="rglru_core",
    )(xz, conv_w, conv_b.reshape(1, c), w_rg.astype(BF16), b_rg.reshape(1, c),
      w_ig.astype(BF16), b_ig.reshape(1, c), lam.reshape(1, c))


def kernel(x, rel_bias, mix_norm, mlp_norm, final_norm, attn_w_qkv, attn_w_o, rec_w_in, rec_conv_w,
           rec_conv_b, rec_w_rg, rec_b_rg, rec_w_ig, rec_b_ig, rec_lambda, rec_w_out, mlp_w1, mlp_w2):
    b, s, d = x.shape
    assert b == 1, "the trunk kernels handle one sequence"
    depth = mix_norm.shape[0]
    xr = x.reshape(s, d)
    xn = _rmsnorm(xr, mix_norm[0])
    mlp_w2_bf16 = mlp_w2.astype(BF16)
    out = None
    for layer in range(depth):
        j = layer // 2
        if layer % 2 == 0:
            qkv = _proj(xn, attn_w_qkv, j, BF16, "attn_qkv_proj",
                        epilogue="scale_leading", scaled_cols=d, scale=HEAD_DIM ** -0.5 * LOG2E)
            mixed = _moba_attention(qkv, rel_bias)
            w_out = attn_w_o
        else:
            xz = _proj(xn, rec_w_in, j, F32, "rec_in_proj")
            mixed = _rglru(xz, rec_conv_w[j], rec_conv_b[j], rec_w_rg[j], rec_b_rg[j],
                           rec_w_ig[j], rec_b_ig[j], rec_lambda[j])
            w_out = rec_w_out
        xr, xn = _resid_proj(mixed, w_out, j, xr, mlp_norm[layer], "mixer_out_proj")
        hidden = _proj(xn, mlp_w1, layer, BF16, "mlp_up_proj", epilogue="relu2")
        if layer + 1 < depth:
            xr, xn = _resid_proj(hidden, mlp_w2_bf16, layer, xr, mix_norm[layer + 1], "mlp_down_proj")
        else:
            out = _resid_proj(hidden, mlp_w2_bf16, layer, xr, final_norm, "mlp_down_proj_final", final=True)
    return out.reshape(b, s, d)
```

```python
import functools
import math

import jax
import jax.numpy as jnp
from jax import lax
from jax.experimental import pallas as pl
from jax.experimental.pallas import tpu as pltpu

HEAD_DIM = 128
MOBA_BLOCK = 256
MOBA_TOPK = 3
NUM_BUCKETS = 32
MAX_DISTANCE = 4096
CONV_WIDTH = 4
LRU_C = 8.0
GATE_BLOCK = 128
EPS = 1e-6
MASKED = -1e30
LOG2E = math.log2(math.e)
ATTN_QUERY_BLOCKS = 4
ATTN_SUM_ROWS = 16

V7X_VMEM_BYTES = 64 * 1024 * 1024
V7X_LANES = 128
V7X_SUBLANES = 8
VMEM_LIMIT_BYTES = V7X_VMEM_BYTES - 8 * 1024 * 1024

F32 = jnp.float32
BF16 = jnp.bfloat16


def _tile(dim, preferred, align):
    if dim <= preferred:
        return dim
    t = (preferred // align) * align
    while t > align and dim % t:
        t -= align
    assert dim % t == 0, (dim, preferred, align)
    return t


def _params(*semantics):
    return pltpu.CompilerParams(dimension_semantics=semantics, vmem_limit_bytes=VMEM_LIMIT_BYTES)


def _rms_rows(x, g):
    y = x * lax.rsqrt(jnp.mean(x * x, axis=-1, keepdims=True) + EPS)
    return y * g


def _norm_kernel(x_ref, g_ref, o_ref):
    o_ref[...] = _rms_rows(x_ref[...], g_ref[...]).astype(o_ref.dtype)


def _rmsnorm(x, g):
    m, d = x.shape
    tm = _tile(m, 512, V7X_SUBLANES)
    return pl.pallas_call(
        _norm_kernel,
        grid=(m // tm,),
        in_specs=[pl.BlockSpec((tm, d), lambda i: (i, 0)), pl.BlockSpec((1, d), lambda i: (0, 0))],
        out_specs=pl.BlockSpec((tm, d), lambda i: (i, 0)),
        out_shape=jax.ShapeDtypeStruct((m, d), BF16),
        compiler_params=_params("arbitrary"),
        name="rmsnorm_in",
    )(x, g.reshape(1, d))


def _proj_kernel(*refs, epilogue, scaled_tiles, scale, side_cast):
    if side_cast:
        a_ref, w_ref, side_in_ref, o_ref, side_out_ref, wb_ref = refs
        side_out_ref[...] = side_in_ref[...].astype(BF16)
    else:
        a_ref, w_ref, o_ref, wb_ref = refs

    @pl.when(pl.program_id(1) == 0)
    def _():
        wb_ref[...] = w_ref[...].astype(BF16)

    acc = jnp.dot(a_ref[...], wb_ref[...], preferred_element_type=F32)
    if epilogue == "relu2":
        acc = jnp.square(jnp.maximum(acc, 0.0))
    elif epilogue == "scale_leading":
        acc = acc * jnp.where(pl.program_id(0) < scaled_tiles, scale, 1.0).astype(F32)
    o_ref[...] = acc.astype(o_ref.dtype)


def _proj(a, w, layer, out_dtype, name, epilogue="none", scaled_cols=0, scale=1.0, side_cast=None):
    m, k = a.shape
    n = w.shape[2]
    tn = _tile(math.gcd(n, scaled_cols) if scaled_cols else n, 1024, V7X_LANES)
    out_bytes = jnp.dtype(out_dtype).itemsize
    side_bytes = 0 if side_cast is None else 2 * (4 + 2) * side_cast[0].shape[1] * side_cast[0].shape[2] // (n // tn)
    fixed_bytes = k * tn * (2 * 4 + 2)
    row_bytes = 2 * k * 2 + 2 * tn * out_bytes + 4 * tn
    tm = _tile(m, min(2048, (VMEM_LIMIT_BYTES - fixed_bytes) // (row_bytes + side_bytes // m + 1)
                      // V7X_LANES * V7X_LANES), V7X_SUBLANES)
    nj, ni = n // tn, m // tm
    kern = functools.partial(_proj_kernel, epilogue=epilogue, scaled_tiles=scaled_cols // tn, scale=scale,
                             side_cast=side_cast is not None)
    in_specs = [pl.BlockSpec((tm, k), lambda j, i: (i, 0)),
                pl.BlockSpec((None, k, tn), lambda j, i: (layer, 0, j))]
    out_specs = [pl.BlockSpec((tm, tn), lambda j, i: (i, j))]
    out_shape = [jax.ShapeDtypeStruct((m, n), out_dtype)]
    operands = [a, w]
    if side_cast is not None:
        w2, layer2 = side_cast
        _, k2, n2 = w2.shape
        assert k2 % (nj * ni) == 0, (k2, nj, ni)
        slab = k2 // (nj * ni)
        in_specs.append(pl.BlockSpec((None, slab, n2), lambda j, i: (layer2, j * ni + i, 0)))
        out_specs.append(pl.BlockSpec((slab, n2), lambda j, i: (j * ni + i, 0)))
        out_shape.append(jax.ShapeDtypeStruct((k2, n2), BF16))
        operands.append(w2)
    outs = pl.pallas_call(
        kern,
        grid=(nj, ni),
        in_specs=in_specs,
        out_specs=out_specs,
        out_shape=out_shape,
        scratch_shapes=[pltpu.VMEM((k, tn), BF16)],
        compiler_params=_params("arbitrary", "arbitrary"),
        name=name,
    )(*operands)
    return outs if side_cast is not None else outs[0]


def _resid_kernel(a_ref, w_ref, res_ref, g_ref, *refs, final, cast_weight):
    n_out = 1 if final else 2
    out_refs = refs[:n_out]
    if cast_weight:
        wb_ref = refs[n_out]

        @pl.when(pl.program_id(0) == 0)
        def _():
            wb_ref[...] = w_ref[...].astype(BF16)
    else:
        wb_ref = w_ref

    x = res_ref[...] + jnp.dot(a_ref[...], wb_ref[...], preferred_element_type=F32)
    y = _rms_rows(x, g_ref[...])
    if final:
        out_refs[0][...] = y
    else:
        out_refs[0][...] = x
        out_refs[1][...] = y.astype(out_refs[1].dtype)


def _resid_proj(a, w, layer, res, g, name, final=False):
    m, k = a.shape
    n = w.shape[2]
    cast_weight = w.dtype != BF16
    weight_bytes = k * n * (w.dtype.itemsize + (2 if cast_weight else 0))
    row_bytes = 2 * (2 * k + 4 * n + (4 * n if final else 6 * n)) + 4 * n
    tm = _tile(m, max(V7X_LANES, (VMEM_LIMIT_BYTES - weight_bytes) // row_bytes // V7X_LANES * V7X_LANES),
               V7X_SUBLANES)
    kern = functools.partial(_resid_kernel, final=final, cast_weight=cast_weight)
    row_spec = pl.BlockSpec((tm, n), lambda i: (i, 0))
    if final:
        out_shape = (jax.ShapeDtypeStruct((m, n), F32),)
        out_specs = (row_spec,)
    else:
        out_shape = (jax.ShapeDtypeStruct((m, n), F32), jax.ShapeDtypeStruct((m, n), BF16))
        out_specs = (row_spec, row_spec)
    outs = pl.pallas_call(
        kern,
        grid=(m // tm,),
        in_specs=[
            pl.BlockSpec((tm, k), lambda i: (i, 0)),
            pl.BlockSpec((None, k, n), lambda i: (layer, 0, 0), pipeline_mode=pl.Buffered(1)),
            row_spec,
            pl.BlockSpec((1, n), lambda i: (0, 0)),
        ],
        out_specs=out_specs,
        out_shape=out_shape,
        scratch_shapes=[pltpu.VMEM((k, n), BF16)] if cast_weight else [],
        compiler_params=_params("arbitrary"),
        name=name,
    )(a, w, res, g.reshape(1, n))
    return outs[0] if final else outs


def _attn_kernel(rb_ref, bkt_ref, q_ref, k_ref, v_ref, o_ref,
                 kmean_ref, kaug_ref, vt_ref, bias_ref, qaug_ref, s0_ref, s1_ref, p0_ref, p1_ref, acc_ref,
                 *, nb, qb, n_sel):
    bs = MOBA_BLOCK
    hd = HEAD_DIM
    qw = qb * bs
    h = pl.program_id(0)
    i0 = pl.program_id(1) * qb
    neg_inf = jnp.float32(-jnp.inf)

    @pl.when(pl.program_id(1) == 0)
    def _per_head_setup():
        bkt = bkt_ref[...]
        bias_row = jnp.zeros(bkt.shape, F32)
        for b in range(NUM_BUCKETS):
            bias_row = jnp.where(bkt == b, rb_ref[b, h] * LOG2E, bias_row)
        for d in range(nb):
            seg = jnp.broadcast_to(bias_row[:, d * bs:(d + 2) * bs], (bs, 2 * bs))
            tile = pltpu.roll(seg, 0, 1, stride=1, stride_axis=0)[:, bs:]
            if d == 0:
                kpos = lax.broadcasted_iota(jnp.int32, tile.shape, 0)
                qpos = lax.broadcasted_iota(jnp.int32, tile.shape, 1)
                tile = jnp.where(kpos <= qpos, tile, MASKED)
            bias_ref[d] = tile
        lane_blk = lax.broadcasted_iota(jnp.int32, (bs, V7X_LANES), 1)
        for j in range(nb):
            kj = k_ref[j * bs:(j + 1) * bs, :]
            kmean_ref[j:j + 1, :] = jnp.mean(kj.astype(F32), axis=0, keepdims=True)
            kaug_ref[j * bs:(j + 1) * bs, :] = jnp.concatenate(
                [kj, jnp.where(lane_blk == j, 1.0, 0.0).astype(BF16)], axis=1)
            ones_row = jnp.where(lax.broadcasted_iota(jnp.int32, (ATTN_SUM_ROWS, bs), 0) == 0, 1.0, 0.0)
            vt_ref[j] = jnp.concatenate([v_ref[j * bs:(j + 1) * bs, :].astype(F32).T, ones_row], axis=0).astype(BF16)

    q = q_ref[...]
    nt = (((1,), (1,)), ((), ()))

    gate = lax.dot_general(kmean_ref[...].astype(BF16), q, nt, preferred_element_type=F32)
    blk = lax.broadcasted_iota(jnp.int32, gate.shape, 0)
    cur = i0 + lax.broadcasted_iota(jnp.int32, gate.shape, 1) // bs
    gate = jnp.where(blk < cur, gate, neg_inf)
    madd = jnp.where(blk == cur, 0.0, MASKED).astype(F32)
    for _ in range(n_sel):
        top = jnp.max(gate, axis=0, keepdims=True)
        idx = jnp.min(jnp.where(gate == top, blk, nb), axis=0, keepdims=True)
        hit = blk == idx
        madd = jnp.where(jnp.logical_and(hit, top > neg_inf), 0.0, madd)
        gate = jnp.where(hit, neg_inf, gate)
    madd = jnp.concatenate([madd, jnp.zeros((V7X_LANES - nb, qw), F32)], axis=0)
    qaug_ref[:, :hd] = q
    qaug_ref[:, hd:] = madd.T.astype(BF16)

    def scores(j):
        kj = kaug_ref[pl.ds(pl.multiple_of(j * bs, bs), bs), :]
        s = lax.dot_general(kj, qaug_ref[...], nt, preferred_element_type=F32)
        bias = jnp.concatenate([bias_ref[jnp.maximum(i0 + a - j, 0)] for a in range(qb)], axis=1)
        return s + bias

    last = i0 + qb - 1

    def key_block(j, m, s_cur_ref, s_next_ref, p_prev_ref, p_cur_ref):
        pv_prev = jnp.dot(vt_ref[jnp.maximum(j - 1, 0)], p_prev_ref[...], preferred_element_type=F32)
        s = s_cur_ref[...]
        m_new = jnp.maximum(m, jnp.max(s, axis=0, keepdims=True))
        p_cur_ref[...] = jnp.exp2(s - m_new).astype(BF16)
        acc_ref[...] = jnp.exp2(m - m_new) * (acc_ref[...] + pv_prev)
        s_next_ref[...] = scores(jnp.minimum(j + 1, last))
        return m_new

    def key_block_group(g, m):
        for u in range(0, qb, 2):
            m = key_block(qb * g + u, m, s0_ref, s1_ref, p1_ref, p0_ref)
            m = key_block(qb * g + u + 1, m, s1_ref, s0_ref, p0_ref, p1_ref)
        return m

    s0_ref[...] = scores(0)
    p1_ref[...] = jnp.zeros(p1_ref.shape, BF16)
    acc_ref[...] = jnp.zeros(acc_ref.shape, F32)
    lax.fori_loop(0, pl.program_id(1) + 1, key_block_group, jnp.full((1, qw), MASKED, F32))
    acc = acc_ref[...] + jnp.dot(vt_ref[last], p1_ref[...], preferred_element_type=F32)
    o_ref[...] = (acc[:hd] / acc[hd:hd + 1]).T.astype(o_ref.dtype)


def _t5_bucket(dist):
    n = jnp.maximum(dist, 0)
    max_exact = NUM_BUCKETS // 2
    nf = jnp.maximum(n, max_exact).astype(F32)
    large = max_exact + (jnp.log(nf / max_exact) / math.log(MAX_DISTANCE / max_exact)
                         * (NUM_BUCKETS - max_exact)).astype(jnp.int32)
    large = jnp.minimum(large, NUM_BUCKETS - 1)
    return jnp.where(n < max_exact, n, large)


def _moba_attention(qkv, rel_bias):
    s, three_d = qkv.shape
    d = three_d // 3
    nh = d // HEAD_DIM
    bs = MOBA_BLOCK
    assert s % bs == 0 and d % HEAD_DIM == 0
    nb = s // bs
    n_sel = min(MOBA_TOPK, nb - 1)
    qb = _tile(nb, ATTN_QUERY_BLOCKS, 2)
    assert nb <= V7X_LANES, "the block one-hot code occupies one lane tile"
    bkt = _t5_bucket(jnp.arange(-bs, s, dtype=jnp.int32)).reshape(1, s + bs)
    kern = functools.partial(_attn_kernel, nb=nb, qb=qb, n_sel=n_sel)
    return pl.pallas_call(
        kern,
        grid=(nh, nb // qb),
        in_specs=[
            pl.BlockSpec(memory_space=pltpu.SMEM),
            pl.BlockSpec((1, s + bs), lambda h, i: (0, 0)),
            pl.BlockSpec((qb * bs, HEAD_DIM), lambda h, i: (i, h)),
            pl.BlockSpec((s, HEAD_DIM), lambda h, i: (0, nh + h)),
            pl.BlockSpec((s, HEAD_DIM), lambda h, i: (0, 2 * nh + h)),
        ],
        out_specs=pl.BlockSpec((qb * bs, HEAD_DIM), lambda h, i: (i, h)),
        out_shape=jax.ShapeDtypeStruct((s, d), BF16),
        scratch_shapes=[
            pltpu.VMEM((nb, HEAD_DIM), F32),
            pltpu.VMEM((s, HEAD_DIM + V7X_LANES), BF16),
            pltpu.VMEM((nb, HEAD_DIM + ATTN_SUM_ROWS, bs), BF16),
            pltpu.VMEM((nb, bs, bs), F32),
            pltpu.VMEM((qb * bs, HEAD_DIM + V7X_LANES), BF16),
            pltpu.VMEM((bs, qb * bs), F32),
            pltpu.VMEM((bs, qb * bs), F32),
            pltpu.VMEM((bs, qb * bs), BF16),
            pltpu.VMEM((bs, qb * bs), BF16),
            pltpu.VMEM((HEAD_DIM + ATTN_SUM_ROWS, qb * bs), F32),
        ],
        compiler_params=_params("arbitrary", "arbitrary"),
        name="moba_attention",
    )(rel_bias, bkt, qkv, qkv, qkv)


def _sigmoid(x):
    return 0.5 * jnp.tanh(0.5 * x) + 0.5


def _rglru_kernel(xz_ref, cw_ref, cb_ref, wrg_ref, brg_ref, wig_ref, big_ref, lam_ref, y_ref,
                  xbuf_ref, xc_ref, a_ref, b_ref, h_ref, *, tt, c):
    pad = V7X_SUBLANES

    @pl.when(pl.program_id(0) == 0)
    def _():
        xbuf_ref[0:pad, :] = jnp.zeros((pad, c), F32)
        h_ref[...] = jnp.zeros((1, c), F32)

    xbuf_ref[pad:pad + tt, :] = xz_ref[:, :c]
    xc = cb_ref[...]
    for j in range(CONV_WIDTH):
        lag = CONV_WIDTH - 1 - j
        xc = xc + xbuf_ref[pad - lag:pad - lag + tt, :] * cw_ref[j:j + 1, :]
    xc_ref[...] = xc
    xbuf_ref[0:pad, :] = xbuf_ref[tt:tt + pad, :]

    for n in range(c // GATE_BLOCK):
        sl = slice(n * GATE_BLOCK, (n + 1) * GATE_BLOCK)
        xcn = xc_ref[:, sl]
        xb = xcn.astype(BF16)
        r = _sigmoid(jnp.dot(xb, wrg_ref[n], preferred_element_type=F32) + brg_ref[:, sl])
        ig = _sigmoid(jnp.dot(xb, wig_ref[n], preferred_element_type=F32) + big_ref[:, sl])
        log_a = (LRU_C * r) * jax.nn.log_sigmoid(lam_ref[:, sl])
        a = jnp.exp(log_a)
        a_ref[:, sl] = a
        w = -jnp.tanh(log_a) * (1.0 + a * a)
        b_ref[:, sl] = jnp.where(w > 0.0, w * lax.rsqrt(w), 0.0) * (ig * xcn)

    def step(t, hprev):
        hnew = a_ref[pl.ds(t, 1), :] * hprev + b_ref[pl.ds(t, 1), :]
        b_ref[pl.ds(t, 1), :] = hnew
        return hnew

    h_ref[...] = lax.fori_loop(0, tt, step, h_ref[...], unroll=8)
    y_ref[...] = (b_ref[...] * jax.nn.gelu(xz_ref[:, c:])).astype(y_ref.dtype)


def _rglru(xz, conv_w, conv_b, w_rg, b_rg, w_ig, b_ig, lam):
    s, two_c = xz.shape
    c = two_c // 2
    assert c % GATE_BLOCK == 0
    ngb = c // GATE_BLOCK
    tt = _tile(s, 256, V7X_SUBLANES)
    kern = functools.partial(_rglru_kernel, tt=tt, c=c)
    vec = pl.BlockSpec((1, c), lambda t: (0, 0))
    gate_w = pl.BlockSpec((ngb, GATE_BLOCK, GATE_BLOCK), lambda t: (0, 0, 0))
    return pl.pallas_call(
        kern,
        grid=(s // tt,),
        in_specs=[
            pl.BlockSpec((tt, two_c), lambda t: (t, 0)),
            pl.BlockSpec((CONV_WIDTH, c), lambda t: (0, 0)),
            vec, gate_w, vec, gate_w, vec, vec,
        ],
        out_specs=pl.BlockSpec((tt, c), lambda t: (t, 0)),
        out_shape=jax.ShapeDtypeStruct((s, c), BF16),
        scratch_shapes=[
            pltpu.VMEM((tt + 2 * V7X_SUBLANES, c), F32),
            pltpu.VMEM((tt, c), F32),
            pltpu.VMEM((tt, c), F32),
            pltpu.VMEM((tt, c), F32),
            pltpu.VMEM((1, c), F32),
        ],
        compiler_params=_params("arbitrary"),
        name="rglru_core",
    )(xz, conv_w, conv_b.reshape(1, c), w_rg.astype(BF16), b_rg.reshape(1, c),
      w_ig.astype(BF16), b_ig.reshape(1, c), lam.reshape(1, c))


def kernel(x, rel_bias, mix_norm, mlp_norm, final_norm, attn_w_qkv, attn_w_o, rec_w_in, rec_conv_w,
           rec_conv_b, rec_w_rg, rec_b_rg, rec_w_ig, rec_b_ig, rec_lambda, rec_w_out, mlp_w1, mlp_w2):
    b, s, d = x.shape
    assert b == 1, "the trunk kernels handle one sequence"
    depth = mix_norm.shape[0]
    xr = x.reshape(s, d)
    xn = _rmsnorm(xr, mix_norm[0])
    out = None
    for layer in range(depth):
        j = layer // 2
        if layer % 2 == 0:
            qkv = _proj(xn, attn_w_qkv, j, BF16, "attn_qkv_proj",
                        epilogue="scale_leading", scaled_cols=d, scale=HEAD_DIM ** -0.5 * LOG2E)
            mixed = _moba_attention(qkv, rel_bias)
            w_out = attn_w_o
        else:
            xz = _proj(xn, rec_w_in, j, F32, "rec_in_proj")
            mixed = _rglru(xz, rec_conv_w[j], rec_conv_b[j], rec_w_rg[j], rec_b_rg[j],
                           rec_w_ig[j], rec_b_ig[j], rec_lambda[j])
            w_out = rec_w_out
        xr, xn = _resid_proj(mixed, w_out, j, xr, mlp_norm[layer], "mixer_out_proj")
        hidden, w2 = _proj(xn, mlp_w1, layer, BF16, "mlp_up_proj", epilogue="relu2", side_cast=(mlp_w2, layer))
        if layer + 1 < depth:
            xr, xn = _resid_proj(hidden, w2[None], 0, xr, mix_norm[layer + 1], "mlp_down_proj")
        else:
            out = _resid_proj(hidden, w2[None], 0, xr, final_norm, "mlp_down_proj_final", final=True)
    return out.reshape(b, s, d)
```

```python
import functools
import math

import jax
import jax.numpy as jnp
from jax import lax
from jax.experimental import pallas as pl
from jax.experimental.pallas import tpu as pltpu

HEAD_DIM = 128
MOBA_BLOCK = 256
MOBA_TOPK = 3
NUM_BUCKETS = 32
MAX_DISTANCE = 4096
CONV_WIDTH = 4
LRU_C = 8.0
GATE_BLOCK = 128
EPS = 1e-6
MASKED = -1e30
LOG2E = math.log2(math.e)
ATTN_QUERY_BLOCKS = 4
ATTN_SUM_ROWS = 16

V7X_VMEM_BYTES = 64 * 1024 * 1024
V7X_LANES = 128
V7X_SUBLANES = 8
VMEM_LIMIT_BYTES = V7X_VMEM_BYTES - 8 * 1024 * 1024

F32 = jnp.float32
BF16 = jnp.bfloat16


def _tile(dim, preferred, align):
    if dim <= preferred:
        return dim
    t = (preferred // align) * align
    while t > align and dim % t:
        t -= align
    assert dim % t == 0, (dim, preferred, align)
    return t


def _params(*semantics):
    return pltpu.CompilerParams(dimension_semantics=semantics, vmem_limit_bytes=VMEM_LIMIT_BYTES)


def _rms_rows(x, g):
    y = x * lax.rsqrt(jnp.mean(x * x, axis=-1, keepdims=True) + EPS)
    return y * g


def _norm_kernel(x_ref, g_ref, o_ref):
    o_ref[...] = _rms_rows(x_ref[...], g_ref[...]).astype(o_ref.dtype)


def _rmsnorm(x, g):
    m, d = x.shape
    tm = _tile(m, 512, V7X_SUBLANES)
    return pl.pallas_call(
        _norm_kernel,
        grid=(m // tm,),
        in_specs=[pl.BlockSpec((tm, d), lambda i: (i, 0)), pl.BlockSpec((1, d), lambda i: (0, 0))],
        out_specs=pl.BlockSpec((tm, d), lambda i: (i, 0)),
        out_shape=jax.ShapeDtypeStruct((m, d), BF16),
        compiler_params=_params("arbitrary"),
        name="rmsnorm_in",
    )(x, g.reshape(1, d))


def _proj_kernel(*refs, epilogue, scaled_tiles, scale, side_cast):
    if side_cast:
        a_ref, w_ref, side_in_ref, o_ref, side_out_ref, wb_ref = refs
        side_out_ref[...] = side_in_ref[...].astype(BF16)
    else:
        a_ref, w_ref, o_ref, wb_ref = refs

    @pl.when(pl.program_id(1) == 0)
    def _():
        wb_ref[...] = w_ref[...].astype(BF16)

    acc = jnp.dot(a_ref[...], wb_ref[...], preferred_element_type=F32)
    if epilogue == "relu2":
        acc = jnp.square(jnp.maximum(acc, 0.0))
    elif epilogue == "scale_leading":
        acc = acc * jnp.where(pl.program_id(0) < scaled_tiles, scale, 1.0).astype(F32)
    o_ref[...] = acc.astype(o_ref.dtype)


def _proj(a, w, layer, out_dtype, name, epilogue="none", scaled_cols=0, scale=1.0, side_cast=None):
    m, k = a.shape
    n = w.shape[2]
    tn = _tile(math.gcd(n, scaled_cols) if scaled_cols else n, 1024, V7X_LANES)
    out_bytes = jnp.dtype(out_dtype).itemsize
    side_bytes = 0 if side_cast is None else 2 * (4 + 2) * side_cast[0].shape[1] * side_cast[0].shape[2] // (n // tn)
    fixed_bytes = k * tn * (2 * 4 + 2)
    row_bytes = 2 * k * 2 + 2 * tn * out_bytes + 4 * tn
    tm = _tile(m, min(2048, (VMEM_LIMIT_BYTES - fixed_bytes) // (row_bytes + side_bytes // m + 1)
                      // V7X_LANES * V7X_LANES), V7X_SUBLANES)
    nj, ni = n // tn, m // tm
    kern = functools.partial(_proj_kernel, epilogue=epilogue, scaled_tiles=scaled_cols // tn, scale=scale,
                             side_cast=side_cast is not None)
    in_specs = [pl.BlockSpec((tm, k), lambda j, i: (i, 0)),
                pl.BlockSpec((None, k, tn), lambda j, i: (layer, 0, j))]
    out_specs = [pl.BlockSpec((tm, tn), lambda j, i: (i, j))]
    out_shape = [jax.ShapeDtypeStruct((m, n), out_dtype)]
    operands = [a, w]
    if side_cast is not None:
        w2, layer2 = side_cast
        _, k2, n2 = w2.shape
        assert k2 % (nj * ni) == 0, (k2, nj, ni)
        slab = k2 // (nj * ni)
        in_specs.append(pl.BlockSpec((None, slab, n2), lambda j, i: (layer2, j * ni + i, 0)))
        out_specs.append(pl.BlockSpec((slab, n2), lambda j, i: (j * ni + i, 0)))
        out_shape.append(jax.ShapeDtypeStruct((k2, n2), BF16))
        operands.append(w2)
    outs = pl.pallas_call(
        kern,
        grid=(nj, ni),
        in_specs=in_specs,
        out_specs=out_specs,
        out_shape=out_shape,
        scratch_shapes=[pltpu.VMEM((k, tn), BF16)],
        compiler_params=_params("arbitrary", "arbitrary"),
        name=name,
    )(*operands)
    return outs if side_cast is not None else outs[0]


def _resid_kernel(a_ref, w_ref, res_ref, g_ref, *refs, final, cast_weight):
    n_out = 1 if final else 2
    out_refs = refs[:n_out]
    if cast_weight:
        wb_ref = refs[n_out]

        @pl.when(pl.program_id(0) == 0)
        def _():
            wb_ref[...] = w_ref[...].astype(BF16)
    else:
        wb_ref = w_ref

    x = res_ref[...] + jnp.dot(a_ref[...], wb_ref[...], preferred_element_type=F32)
    y = _rms_rows(x, g_ref[...])
    if final:
        out_refs[0][...] = y
    else:
        out_refs[0][...] = x
        out_refs[1][...] = y.astype(out_refs[1].dtype)


def _resid_proj(a, w, layer, res, g, name, final=False):
    m, k = a.shape
    n = w.shape[2]
    cast_weight = w.dtype != BF16
    weight_bytes = k * n * (w.dtype.itemsize + (2 if cast_weight else 0))
    row_bytes = 2 * (2 * k + 4 * n + (4 * n if final else 6 * n)) + 4 * n
    tm = _tile(m, max(V7X_LANES, (VMEM_LIMIT_BYTES - weight_bytes) // row_bytes // V7X_LANES * V7X_LANES),
               V7X_SUBLANES)
    kern = functools.partial(_resid_kernel, final=final, cast_weight=cast_weight)
    row_spec = pl.BlockSpec((tm, n), lambda i: (i, 0))
    if final:
        out_shape = (jax.ShapeDtypeStruct((m, n), F32),)
        out_specs = (row_spec,)
    else:
        out_shape = (jax.ShapeDtypeStruct((m, n), F32), jax.ShapeDtypeStruct((m, n), BF16))
        out_specs = (row_spec, row_spec)
    outs = pl.pallas_call(
        kern,
        grid=(m // tm,),
        in_specs=[
            pl.BlockSpec((tm, k), lambda i: (i, 0)),
            pl.BlockSpec((None, k, n), lambda i: (layer, 0, 0), pipeline_mode=pl.Buffered(1)),
            row_spec,
            pl.BlockSpec((1, n), lambda i: (0, 0)),
        ],
        out_specs=out_specs,
        out_shape=out_shape,
        scratch_shapes=[pltpu.VMEM((k, n), BF16)] if cast_weight else [],
        compiler_params=_params("arbitrary"),
        name=name,
    )(a, w, res, g.reshape(1, n))
    return outs[0] if final else outs


def _attn_kernel(rb_ref, bkt_ref, q_ref, k_ref, v_ref, o_ref,
                 kmean_ref, kaug_ref, vt_ref, bias_ref, qaug_ref, s0_ref, s1_ref, p0_ref, p1_ref, acc_ref,
                 *, nb, qb, n_sel):
    bs = MOBA_BLOCK
    hd = HEAD_DIM
    qw = qb * bs
    h = pl.program_id(0)
    i0 = pl.program_id(1) * qb
    neg_inf = jnp.float32(-jnp.inf)

    @pl.when(pl.program_id(1) == 0)
    def _per_head_setup():
        bkt = bkt_ref[...]
        bias_row = jnp.zeros(bkt.shape, F32)
        for b in range(NUM_BUCKETS):
            bias_row = jnp.where(bkt == b, rb_ref[b, h] * LOG2E, bias_row)
        for d in range(nb):
            seg = jnp.broadcast_to(bias_row[:, d * bs:(d + 2) * bs], (bs, 2 * bs))
            tile = pltpu.roll(seg, 0, 1, stride=1, stride_axis=0)[:, bs:]
            if d == 0:
                kpos = lax.broadcasted_iota(jnp.int32, tile.shape, 0)
                qpos = lax.broadcasted_iota(jnp.int32, tile.shape, 1)
                tile = jnp.where(kpos <= qpos, tile, MASKED)
            bias_ref[d] = tile
        lane_blk = lax.broadcasted_iota(jnp.int32, (bs, V7X_LANES), 1)
        for j in range(nb):
            kj = k_ref[j * bs:(j + 1) * bs, :]
            kmean_ref[j:j + 1, :] = jnp.mean(kj.astype(F32), axis=0, keepdims=True)
            kaug_ref[j * bs:(j + 1) * bs, :] = jnp.concatenate(
                [kj, jnp.where(lane_blk == j, 1.0, 0.0).astype(BF16)], axis=1)
            ones_row = jnp.where(lax.broadcasted_iota(jnp.int32, (ATTN_SUM_ROWS, bs), 0) == 0, 1.0, 0.0)
            vt_ref[j] = jnp.concatenate([v_ref[j * bs:(j + 1) * bs, :].astype(F32).T, ones_row], axis=0).astype(BF16)

    qt = q_ref[...].astype(F32).T.astype(BF16)
    qaug_ref[:hd, :] = qt

    gate = jnp.dot(kmean_ref[...].astype(BF16), qt, preferred_element_type=F32)
    blk = lax.broadcasted_iota(jnp.int32, gate.shape, 0)
    cur = i0 + lax.broadcasted_iota(jnp.int32, gate.shape, 1) // bs
    gate = jnp.where(blk < cur, gate, neg_inf)
    madd = jnp.where(blk == cur, 0.0, MASKED).astype(F32)
    for _ in range(n_sel):
        top = jnp.max(gate, axis=0, keepdims=True)
        idx = jnp.min(jnp.where(gate == top, blk, nb), axis=0, keepdims=True)
        hit = blk == idx
        madd = jnp.where(jnp.logical_and(hit, top > neg_inf), 0.0, madd)
        gate = jnp.where(hit, neg_inf, gate)
    qaug_ref[hd:, :] = jnp.concatenate([madd, jnp.zeros((V7X_LANES - nb, qw), F32)], axis=0).astype(BF16)

    def scores(j):
        kj = kaug_ref[pl.ds(pl.multiple_of(j * bs, bs), bs), :]
        s = jnp.dot(kj, qaug_ref[...], preferred_element_type=F32)
        bias = jnp.concatenate([bias_ref[jnp.maximum(i0 + a - j, 0)] for a in range(qb)], axis=1)
        return s + bias

    last = i0 + qb - 1

    def key_block(j, m, s_cur_ref, s_next_ref, p_prev_ref, p_cur_ref):
        pv_prev = jnp.dot(vt_ref[jnp.maximum(j - 1, 0)], p_prev_ref[...], preferred_element_type=F32)
        s = s_cur_ref[...]
        m_new = jnp.maximum(m, jnp.max(s, axis=0, keepdims=True))
        p_cur_ref[...] = jnp.exp2(s - m_new).astype(BF16)
        acc_ref[...] = jnp.exp2(m - m_new) * (acc_ref[...] + pv_prev)
        s_next_ref[...] = scores(jnp.minimum(j + 1, last))
        return m_new

    def key_block_group(g, m):
        for u in range(0, qb, 2):
            m = key_block(qb * g + u, m, s0_ref, s1_ref, p1_ref, p0_ref)
            m = key_block(qb * g + u + 1, m, s1_ref, s0_ref, p0_ref, p1_ref)
        return m

    s0_ref[...] = scores(0)
    p1_ref[...] = jnp.zeros(p1_ref.shape, BF16)
    acc_ref[...] = jnp.zeros(acc_ref.shape, F32)
    lax.fori_loop(0, pl.program_id(1) + 1, key_block_group, jnp.full((1, qw), MASKED, F32))
    acc = acc_ref[...] + jnp.dot(vt_ref[last], p1_ref[...], preferred_element_type=F32)
    o_ref[...] = (acc[:hd] / acc[hd:hd + 1]).T.astype(o_ref.dtype)


def _t5_bucket(dist):
    n = jnp.maximum(dist, 0)
    max_exact = NUM_BUCKETS // 2
    nf = jnp.maximum(n, max_exact).astype(F32)
    large = max_exact + (jnp.log(nf / max_exact) / math.log(MAX_DISTANCE / max_exact)
                         * (NUM_BUCKETS - max_exact)).astype(jnp.int32)
    large = jnp.minimum(large, NUM_BUCKETS - 1)
    return jnp.where(n < max_exact, n, large)


def _moba_attention(qkv, rel_bias):
    s, three_d = qkv.shape
    d = three_d // 3
    nh = d // HEAD_DIM
    bs = MOBA_BLOCK
    assert s % bs == 0 and d % HEAD_DIM == 0
    nb = s // bs
    n_sel = min(MOBA_TOPK, nb - 1)
    qb = _tile(nb, ATTN_QUERY_BLOCKS, 2)
    assert nb <= V7X_LANES, "the block one-hot code occupies one lane tile"
    bkt = _t5_bucket(jnp.arange(-bs, s, dtype=jnp.int32)).reshape(1, s + bs)
    kern = functools.partial(_attn_kernel, nb=nb, qb=qb, n_sel=n_sel)
    return pl.pallas_call(
        kern,
        grid=(nh, nb // qb),
        in_specs=[
            pl.BlockSpec(memory_space=pltpu.SMEM),
            pl.BlockSpec((1, s + bs), lambda h, i: (0, 0)),
            pl.BlockSpec((qb * bs, HEAD_DIM), lambda h, i: (i, h)),
            pl.BlockSpec((s, HEAD_DIM), lambda h, i: (0, nh + h)),
            pl.BlockSpec((s, HEAD_DIM), lambda h, i: (0, 2 * nh + h)),
        ],
        out_specs=pl.BlockSpec((qb * bs, HEAD_DIM), lambda h, i: (i, h)),
        out_shape=jax.ShapeDtypeStruct((s, d), BF16),
        scratch_shapes=[
            pltpu.VMEM((nb, HEAD_DIM), F32),
            pltpu.VMEM((s, HEAD_DIM + V7X_LANES), BF16),
            pltpu.VMEM((nb, HEAD_DIM + ATTN_SUM_ROWS, bs), BF16),
            pltpu.VMEM((nb, bs, bs), F32),
            pltpu.VMEM((HEAD_DIM + V7X_LANES, qb * bs), BF16),
            pltpu.VMEM((bs, qb * bs), F32),
            pltpu.VMEM((bs, qb * bs), F32),
            pltpu.VMEM((bs, qb * bs), BF16),
            pltpu.VMEM((bs, qb * bs), BF16),
            pltpu.VMEM((HEAD_DIM + ATTN_SUM_ROWS, qb * bs), F32),
        ],
        compiler_params=_params("arbitrary", "arbitrary"),
        name="moba_attention",
    )(rel_bias, bkt, qkv, qkv, qkv)


def _sigmoid(x):
    return 0.5 * jnp.tanh(0.5 * x) + 0.5


def _rglru_kernel(xz_ref, cw_ref, cb_ref, wrg_ref, brg_ref, wig_ref, big_ref, lam_ref, y_ref,
                  xbuf_ref, xc_ref, a_ref, b_ref, h_ref, *, tt, c):
    pad = V7X_SUBLANES

    @pl.when(pl.program_id(0) == 0)
    def _():
        xbuf_ref[0:pad, :] = jnp.zeros((pad, c), F32)
        h_ref[...] = jnp.zeros((1, c), F32)

    xbuf_ref[pad:pad + tt, :] = xz_ref[:, :c]
    xc = cb_ref[...]
    for j in range(CONV_WIDTH):
        lag = CONV_WIDTH - 1 - j
        xc = xc + xbuf_ref[pad - lag:pad - lag + tt, :] * cw_ref[j:j + 1, :]
    xc_ref[...] = xc
    xbuf_ref[0:pad, :] = xbuf_ref[tt:tt + pad, :]

    for n in range(c // GATE_BLOCK):
        sl = slice(n * GATE_BLOCK, (n + 1) * GATE_BLOCK)
        xcn = xc_ref[:, sl]
        xb = xcn.astype(BF16)
        r = _sigmoid(jnp.dot(xb, wrg_ref[n], preferred_element_type=F32) + brg_ref[:, sl])
        ig = _sigmoid(jnp.dot(xb, wig_ref[n], preferred_element_type=F32) + big_ref[:, sl])
        log_a = (LRU_C * r) * jax.nn.log_sigmoid(lam_ref[:, sl])
        a = jnp.exp(log_a)
        a_ref[:, sl] = a
        w = -jnp.tanh(log_a) * (1.0 + a * a)
        b_ref[:, sl] = jnp.where(w > 0.0, w * lax.rsqrt(w), 0.0) * (ig * xcn)

    def step(t, hprev):
        hnew = a_ref[pl.ds(t, 1), :] * hprev + b_ref[pl.ds(t, 1), :]
        b_ref[pl.ds(t, 1), :] = hnew
        return hnew

    h_ref[...] = lax.fori_loop(0, tt, step, h_ref[...], unroll=8)
    y_ref[...] = (b_ref[...] * jax.nn.gelu(xz_ref[:, c:])).astype(y_ref.dtype)


def _rglru(xz, conv_w, conv_b, w_rg, b_rg, w_ig, b_ig, lam):
    s, two_c = xz.shape
    c = two_c // 2
    assert c % GATE_BLOCK == 0
    ngb = c // GATE_BLOCK
    tt = _tile(s, 256, V7X_SUBLANES)
    kern = functools.partial(_rglru_kernel, tt=tt, c=c)
    vec = pl.BlockSpec((1, c), lambda t: (0, 0))
    gate_w = pl.BlockSpec((ngb, GATE_BLOCK, GATE_BLOCK), lambda t: (0, 0, 0))
    return pl.pallas_call(
        kern,
        grid=(s // tt,),
        in_specs=[
            pl.BlockSpec((tt, two_c), lambda t: (t, 0)),
            pl.BlockSpec((CONV_WIDTH, c), lambda t: (0, 0)),
            vec, gate_w, vec, gate_w, vec, vec,
        ],
        out_specs=pl.BlockSpec((tt, c), lambda t: (t, 0)),
        out_shape=jax.ShapeDtypeStruct((s, c), BF16),
        scratch_shapes=[
            pltpu.VMEM((tt + 2 * V7X_SUBLANES, c), F32),
            pltpu.VMEM((tt, c), F32),
            pltpu.VMEM((tt, c), F32),
            pltpu.VMEM((tt, c), F32),
            pltpu.VMEM((1, c), F32),
        ],
        compiler_params=_params("arbitrary"),
        name="rglru_core",
    )(xz, conv_w, conv_b.reshape(1, c), w_rg.astype(BF16), b_rg.reshape(1, c),
      w_ig.astype(BF16), b_ig.reshape(1, c), lam.reshape(1, c))


def kernel(x, rel_bias, mix_norm, mlp_norm, final_norm, attn_w_qkv, attn_w_o, rec_w_in, rec_conv_w,
           rec_conv_b, rec_w_rg, rec_b_rg, rec_w_ig, rec_b_ig, rec_lambda, rec_w_out, mlp_w1, mlp_w2):
    b, s, d = x.shape
    assert b == 1, "the trunk kernels handle one sequence"
    depth = mix_norm.shape[0]
    xr = x.reshape(s, d)
    xn = _rmsnorm(xr, mix_norm[0])
    out = None
    for layer in range(depth):
        j = layer // 2
        if layer % 2 == 0:
            qkv = _proj(xn, attn_w_qkv, j, BF16, "attn_qkv_proj",
                        epilogue="scale_leading", scaled_cols=d, scale=HEAD_DIM ** -0.5 * LOG2E)
            mixed = _moba_attention(qkv, rel_bias)
            w_out = attn_w_o
        else:
            xz = _proj(xn, rec_w_in, j, F32, "rec_in_proj")
            mixed = _rglru(xz, rec_conv_w[j], rec_conv_b[j], rec_w_rg[j], rec_b_rg[j],
                           rec_w_ig[j], rec_b_ig[j], rec_lambda[j])
            w_out = rec_w_out
        xr, xn = _resid_proj(mixed, w_out, j, xr, mlp_norm[layer], "mixer_out_proj")
        hidden, w2 = _proj(xn, mlp_w1, layer, BF16, "mlp_up_proj", epilogue="relu2", side_cast=(mlp_w2, layer))
        if layer + 1 < depth:
            xr, xn = _resid_proj(hidden, w2[None], 0, xr, mix_norm[layer + 1], "mlp_down_proj")
        else:
            out = _resid_proj(hidden, w2[None], 0, xr, final_norm, "mlp_down_proj_final", final=True)
    return out.reshape(b, s, d)
```

```python
import functools
import math

import jax
import jax.numpy as jnp
from jax import lax
from jax.experimental import pallas as pl
from jax.experimental.pallas import tpu as pltpu

HEAD_DIM = 128
MOBA_BLOCK = 256
MOBA_TOPK = 3
NUM_BUCKETS = 32
MAX_DISTANCE = 4096
CONV_WIDTH = 4
LRU_C = 8.0
GATE_BLOCK = 128
EPS = 1e-6
MASKED = -1e30
LOG2E = math.log2(math.e)
ATTN_QUERY_BLOCKS = 4
ATTN_SUM_ROWS = 16

V7X_VMEM_BYTES = 64 * 1024 * 1024
V7X_LANES = 128
V7X_SUBLANES = 8
VMEM_LIMIT_BYTES = V7X_VMEM_BYTES - 8 * 1024 * 1024

F32 = jnp.float32
BF16 = jnp.bfloat16


def _tile(dim, preferred, align):
    if dim <= preferred:
        return dim
    t = (preferred // align) * align
    while t > align and dim % t:
        t -= align
    assert dim % t == 0, (dim, preferred, align)
    return t


def _params(*semantics):
    return pltpu.CompilerParams(dimension_semantics=semantics, vmem_limit_bytes=VMEM_LIMIT_BYTES)


def _rms_rows(x, g):
    y = x * lax.rsqrt(jnp.mean(x * x, axis=-1, keepdims=True) + EPS)
    return y * g


def _norm_kernel(x_ref, g_ref, o_ref):
    o_ref[...] = _rms_rows(x_ref[...], g_ref[...]).astype(o_ref.dtype)


def _rmsnorm(x, g):
    m, d = x.shape
    tm = _tile(m, 512, V7X_SUBLANES)
    return pl.pallas_call(
        _norm_kernel,
        grid=(m // tm,),
        in_specs=[pl.BlockSpec((tm, d), lambda i: (i, 0)), pl.BlockSpec((1, d), lambda i: (0, 0))],
        out_specs=pl.BlockSpec((tm, d), lambda i: (i, 0)),
        out_shape=jax.ShapeDtypeStruct((m, d), BF16),
        compiler_params=_params("arbitrary"),
        name="rmsnorm_in",
    )(x, g.reshape(1, d))


def _side_cast_specs(w, layer, n_steps, step_index):
    _, k, n = w.shape
    slab = k // n_steps
    assert slab * n_steps == k and slab % (2 * V7X_SUBLANES) == 0, (k, n_steps)
    return (pl.BlockSpec((None, slab, n), lambda *g: (layer, step_index(*g), 0)),
            pl.BlockSpec((slab, n), lambda *g: (step_index(*g), 0)),
            jax.ShapeDtypeStruct((k, n), BF16))


def _proj_kernel(a_ref, w_ref, o_ref, wb_ref, *, epilogue, scaled_tiles, scale):
    @pl.when(pl.program_id(1) == 0)
    def _():
        wb_ref[...] = w_ref[...].astype(BF16)

    acc = jnp.dot(a_ref[...], wb_ref[...], preferred_element_type=F32)
    if epilogue == "relu2":
        acc = jnp.square(jnp.maximum(acc, 0.0))
    elif epilogue == "scale_leading":
        acc = acc * jnp.where(pl.program_id(0) < scaled_tiles, scale, 1.0).astype(F32)
    o_ref[...] = acc.astype(o_ref.dtype)


def _proj(a, w, layer, out_dtype, name, epilogue="none", scaled_cols=0, scale=1.0):
    m, k = a.shape
    n = w.shape[2]
    tn = _tile(math.gcd(n, scaled_cols) if scaled_cols else n, 1024, V7X_LANES)
    fixed_bytes = k * tn * (2 * 4 + 2)
    row_bytes = 2 * k * 2 + 2 * tn * jnp.dtype(out_dtype).itemsize + 4 * tn
    tm = _tile(m, min(2048, (VMEM_LIMIT_BYTES - fixed_bytes) // row_bytes // V7X_LANES * V7X_LANES), V7X_SUBLANES)
    kern = functools.partial(_proj_kernel, epilogue=epilogue, scaled_tiles=scaled_cols // tn, scale=scale)
    return pl.pallas_call(
        kern,
        grid=(n // tn, m // tm),
        in_specs=[pl.BlockSpec((tm, k), lambda j, i: (i, 0)),
                  pl.BlockSpec((None, k, tn), lambda j, i: (layer, 0, j))],
        out_specs=pl.BlockSpec((tm, tn), lambda j, i: (i, j)),
        out_shape=jax.ShapeDtypeStruct((m, n), out_dtype),
        scratch_shapes=[pltpu.VMEM((k, tn), BF16)],
        compiler_params=_params("arbitrary", "arbitrary"),
        name=name,
    )(a, w)


def _resid_kernel(a_ref, w_ref, res_ref, g_ref, *refs, final, cast_weight):
    n_out = 1 if final else 2
    out_refs = refs[:n_out]
    if cast_weight:
        wb_ref = refs[n_out]

        @pl.when(pl.program_id(0) == 0)
        def _():
            wb_ref[...] = w_ref[...].astype(BF16)
    else:
        wb_ref = w_ref

    x = res_ref[...] + jnp.dot(a_ref[...], wb_ref[...], preferred_element_type=F32)
    y = _rms_rows(x, g_ref[...])
    if final:
        out_refs[0][...] = y
    else:
        out_refs[0][...] = x
        out_refs[1][...] = y.astype(out_refs[1].dtype)


def _resid_proj(a, w, layer, res, g, name, final=False):
    m, k = a.shape
    n = w.shape[2]
    cast_weight = w.dtype != BF16
    weight_bytes = k * n * (w.dtype.itemsize + (2 if cast_weight else 0))
    row_bytes = 2 * (2 * k + 4 * n + (4 * n if final else 6 * n)) + 4 * n
    tm = _tile(m, max(V7X_LANES, (VMEM_LIMIT_BYTES - weight_bytes) // row_bytes // V7X_LANES * V7X_LANES),
               V7X_SUBLANES)
    kern = functools.partial(_resid_kernel, final=final, cast_weight=cast_weight)
    row_spec = pl.BlockSpec((tm, n), lambda i: (i, 0))
    if final:
        out_shape = (jax.ShapeDtypeStruct((m, n), F32),)
        out_specs = (row_spec,)
    else:
        out_shape = (jax.ShapeDtypeStruct((m, n), F32), jax.ShapeDtypeStruct((m, n), BF16))
        out_specs = (row_spec, row_spec)
    outs = pl.pallas_call(
        kern,
        grid=(m // tm,),
        in_specs=[
            pl.BlockSpec((tm, k), lambda i: (i, 0)),
            pl.BlockSpec((None, k, n), lambda i: (layer, 0, 0), pipeline_mode=pl.Buffered(1)),
            row_spec,
            pl.BlockSpec((1, n), lambda i: (0, 0)),
        ],
        out_specs=out_specs,
        out_shape=out_shape,
        scratch_shapes=[pltpu.VMEM((k, n), BF16)] if cast_weight else [],
        compiler_params=_params("arbitrary"),
        name=name,
    )(a, w, res, g.reshape(1, n))
    return outs[0] if final else outs


def _attn_kernel(rb_ref, bkt_ref, q_ref, k_ref, v_ref, side_in_ref, o_ref, side_out_ref,
                 kmean_ref, kaug_ref, vt_ref, bias_ref, qaug_ref, s0_ref, s1_ref, p0_ref, p1_ref, acc_ref,
                 *, nb, qb, n_sel):
    bs = MOBA_BLOCK
    hd = HEAD_DIM
    qw = qb * bs
    h = pl.program_id(0)
    i0 = pl.program_id(1) * qb
    neg_inf = jnp.float32(-jnp.inf)
    side_out_ref[...] = side_in_ref[...].astype(BF16)

    @pl.when(pl.program_id(1) == 0)
    def _per_head_setup():
        bkt = bkt_ref[...]
        bias_row = jnp.zeros(bkt.shape, F32)
        for b in range(NUM_BUCKETS):
            bias_row = jnp.where(bkt == b, rb_ref[b, h] * LOG2E, bias_row)
        for d in range(nb):
            seg = jnp.broadcast_to(bias_row[:, d * bs:(d + 2) * bs], (bs, 2 * bs))
            tile = pltpu.roll(seg, 0, 1, stride=1, stride_axis=0)[:, bs:]
            if d == 0:
                kpos = lax.broadcasted_iota(jnp.int32, tile.shape, 0)
                qpos = lax.broadcasted_iota(jnp.int32, tile.shape, 1)
                tile = jnp.where(kpos <= qpos, tile, MASKED)
            bias_ref[d] = tile
        lane_blk = lax.broadcasted_iota(jnp.int32, (bs, V7X_LANES), 1)
        for j in range(nb):
            kj = k_ref[j * bs:(j + 1) * bs, :]
            kmean_ref[j:j + 1, :] = jnp.mean(kj.astype(F32), axis=0, keepdims=True)
            kaug_ref[j * bs:(j + 1) * bs, :] = jnp.concatenate(
                [kj, jnp.where(lane_blk == j, 1.0, 0.0).astype(BF16)], axis=1)
            ones_row = jnp.where(lax.broadcasted_iota(jnp.int32, (ATTN_SUM_ROWS, bs), 0) == 0, 1.0, 0.0)
            vt_ref[j] = jnp.concatenate([v_ref[j * bs:(j + 1) * bs, :].astype(F32).T, ones_row], axis=0).astype(BF16)

    qt = q_ref[...].astype(F32).T.astype(BF16)
    qaug_ref[:hd, :] = qt

    gate = jnp.dot(kmean_ref[...].astype(BF16), qt, preferred_element_type=F32)
    blk = lax.broadcasted_iota(jnp.int32, gate.shape, 0)
    cur = i0 + lax.broadcasted_iota(jnp.int32, gate.shape, 1) // bs
    gate = jnp.where(blk < cur, gate, neg_inf)
    madd = jnp.where(blk == cur, 0.0, MASKED).astype(F32)
    for _ in range(n_sel):
        top = jnp.max(gate, axis=0, keepdims=True)
        idx = jnp.min(jnp.where(gate == top, blk, nb), axis=0, keepdims=True)
        hit = blk == idx
        madd = jnp.where(jnp.logical_and(hit, top > neg_inf), 0.0, madd)
        gate = jnp.where(hit, neg_inf, gate)
    qaug_ref[hd:, :] = jnp.concatenate([madd, jnp.zeros((V7X_LANES - nb, qw), F32)], axis=0).astype(BF16)

    def scores(j):
        kj = kaug_ref[pl.ds(pl.multiple_of(j * bs, bs), bs), :]
        s = jnp.dot(kj, qaug_ref[...], preferred_element_type=F32)
        bias = jnp.concatenate([bias_ref[jnp.maximum(i0 + a - j, 0)] for a in range(qb)], axis=1)
        return s + bias

    last = i0 + qb - 1

    def key_block(j, m, s_cur_ref, s_next_ref, p_prev_ref, p_cur_ref):
        pv_prev = jnp.dot(vt_ref[jnp.maximum(j - 1, 0)], p_prev_ref[...], preferred_element_type=F32)
        s = s_cur_ref[...]
        m_new = jnp.maximum(m, jnp.max(s, axis=0, keepdims=True))
        p_cur_ref[...] = jnp.exp2(s - m_new).astype(BF16)
        acc_ref[...] = jnp.exp2(m - m_new) * (acc_ref[...] + pv_prev)
        s_next_ref[...] = scores(jnp.minimum(j + 1, last))
        return m_new

    def key_block_group(g, m):
        for u in range(0, qb, 2):
            m = key_block(qb * g + u, m, s0_ref, s1_ref, p1_ref, p0_ref)
            m = key_block(qb * g + u + 1, m, s1_ref, s0_ref, p0_ref, p1_ref)
        return m

    s0_ref[...] = scores(0)
    p1_ref[...] = jnp.zeros(p1_ref.shape, BF16)
    acc_ref[...] = jnp.zeros(acc_ref.shape, F32)
    lax.fori_loop(0, pl.program_id(1) + 1, key_block_group, jnp.full((1, qw), MASKED, F32))
    acc = acc_ref[...] + jnp.dot(vt_ref[last], p1_ref[...], preferred_element_type=F32)
    o_ref[...] = (acc[:hd] / acc[hd:hd + 1]).T.astype(o_ref.dtype)


def _t5_bucket(dist):
    n = jnp.maximum(dist, 0)
    max_exact = NUM_BUCKETS // 2
    nf = jnp.maximum(n, max_exact).astype(F32)
    large = max_exact + (jnp.log(nf / max_exact) / math.log(MAX_DISTANCE / max_exact)
                         * (NUM_BUCKETS - max_exact)).astype(jnp.int32)
    large = jnp.minimum(large, NUM_BUCKETS - 1)
    return jnp.where(n < max_exact, n, large)


def _moba_attention(qkv, rel_bias, side_w, side_layer):
    s, three_d = qkv.shape
    d = three_d // 3
    nh = d // HEAD_DIM
    bs = MOBA_BLOCK
    assert s % bs == 0 and d % HEAD_DIM == 0
    nb = s // bs
    n_sel = min(MOBA_TOPK, nb - 1)
    qb = _tile(nb, ATTN_QUERY_BLOCKS, 2)
    assert nb <= V7X_LANES, "the block one-hot code occupies one lane tile"
    bkt = _t5_bucket(jnp.arange(-bs, s, dtype=jnp.int32)).reshape(1, s + bs)
    kern = functools.partial(_attn_kernel, nb=nb, qb=qb, n_sel=n_sel)
    n_ib = nb // qb
    side_in, side_out, side_shape = _side_cast_specs(side_w, side_layer, nh * n_ib, lambda h, i: h * n_ib + i)
    return pl.pallas_call(
        kern,
        grid=(nh, n_ib),
        in_specs=[
            pl.BlockSpec(memory_space=pltpu.SMEM),
            pl.BlockSpec((1, s + bs), lambda h, i: (0, 0)),
            pl.BlockSpec((qb * bs, HEAD_DIM), lambda h, i: (i, h)),
            pl.BlockSpec((s, HEAD_DIM), lambda h, i: (0, nh + h)),
            pl.BlockSpec((s, HEAD_DIM), lambda h, i: (0, 2 * nh + h)),
            side_in,
        ],
        out_specs=(pl.BlockSpec((qb * bs, HEAD_DIM), lambda h, i: (i, h)), side_out),
        out_shape=(jax.ShapeDtypeStruct((s, d), BF16), side_shape),
        scratch_shapes=[
            pltpu.VMEM((nb, HEAD_DIM), F32),
            pltpu.VMEM((s, HEAD_DIM + V7X_LANES), BF16),
            pltpu.VMEM((nb, HEAD_DIM + ATTN_SUM_ROWS, bs), BF16),
            pltpu.VMEM((nb, bs, bs), F32),
            pltpu.VMEM((HEAD_DIM + V7X_LANES, qb * bs), BF16),
            pltpu.VMEM((bs, qb * bs), F32),
            pltpu.VMEM((bs, qb * bs), F32),
            pltpu.VMEM((bs, qb * bs), BF16),
            pltpu.VMEM((bs, qb * bs), BF16),
            pltpu.VMEM((HEAD_DIM + ATTN_SUM_ROWS, qb * bs), F32),
        ],
        compiler_params=_params("arbitrary", "arbitrary"),
        name="moba_attention",
    )(rel_bias, bkt, qkv, qkv, qkv, side_w)


def _sigmoid(x):
    return 0.5 * jnp.tanh(0.5 * x) + 0.5


def _rglru_kernel(xz_ref, cw_ref, cb_ref, wrg_ref, brg_ref, wig_ref, big_ref, lam_ref, side_in_ref,
                  y_ref, side_out_ref, xbuf_ref, xc_ref, a_ref, b_ref, h_ref, *, tt, c):
    pad = V7X_SUBLANES
    side_out_ref[...] = side_in_ref[...].astype(BF16)

    @pl.when(pl.program_id(0) == 0)
    def _():
        xbuf_ref[0:pad, :] = jnp.zeros((pad, c), F32)
        h_ref[...] = jnp.zeros((1, c), F32)

    xbuf_ref[pad:pad + tt, :] = xz_ref[:, :c]
    xc = cb_ref[...]
    for j in range(CONV_WIDTH):
        lag = CONV_WIDTH - 1 - j
        xc = xc + xbuf_ref[pad - lag:pad - lag + tt, :] * cw_ref[j:j + 1, :]
    xc_ref[...] = xc
    xbuf_ref[0:pad, :] = xbuf_ref[tt:tt + pad, :]

    for n in range(c // GATE_BLOCK):
        sl = slice(n * GATE_BLOCK, (n + 1) * GATE_BLOCK)
        xcn = xc_ref[:, sl]
        xb = xcn.astype(BF16)
        r = _sigmoid(jnp.dot(xb, wrg_ref[n], preferred_element_type=F32) + brg_ref[:, sl])
        ig = _sigmoid(jnp.dot(xb, wig_ref[n], preferred_element_type=F32) + big_ref[:, sl])
        log_a = (LRU_C * r) * jax.nn.log_sigmoid(lam_ref[:, sl])
        a = jnp.exp(log_a)
        a_ref[:, sl] = a
        w = -jnp.tanh(log_a) * (1.0 + a * a)
        b_ref[:, sl] = jnp.where(w > 0.0, w * lax.rsqrt(w), 0.0) * (ig * xcn)

    def step(t, hprev):
        hnew = a_ref[pl.ds(t, 1), :] * hprev + b_ref[pl.ds(t, 1), :]
        b_ref[pl.ds(t, 1), :] = hnew
        return hnew

    h_ref[...] = lax.fori_loop(0, tt, step, h_ref[...], unroll=8)
    y_ref[...] = (b_ref[...] * jax.nn.gelu(xz_ref[:, c:])).astype(y_ref.dtype)


def _rglru(xz, conv_w, conv_b, w_rg, b_rg, w_ig, b_ig, lam, side_w, side_layer):
    s, two_c = xz.shape
    c = two_c // 2
    assert c % GATE_BLOCK == 0
    ngb = c // GATE_BLOCK
    tt = _tile(s, 256, V7X_SUBLANES)
    kern = functools.partial(_rglru_kernel, tt=tt, c=c)
    vec = pl.BlockSpec((1, c), lambda t: (0, 0))
    gate_w = pl.BlockSpec((ngb, GATE_BLOCK, GATE_BLOCK), lambda t: (0, 0, 0))
    side_in, side_out, side_shape = _side_cast_specs(side_w, side_layer, s // tt, lambda t: t)
    return pl.pallas_call(
        kern,
        grid=(s // tt,),
        in_specs=[
            pl.BlockSpec((tt, two_c), lambda t: (t, 0)),
            pl.BlockSpec((CONV_WIDTH, c), lambda t: (0, 0)),
            vec, gate_w, vec, gate_w, vec, vec, side_in,
        ],
        out_specs=(pl.BlockSpec((tt, c), lambda t: (t, 0)), side_out),
        out_shape=(jax.ShapeDtypeStruct((s, c), BF16), side_shape),
        scratch_shapes=[
            pltpu.VMEM((tt + 2 * V7X_SUBLANES, c), F32),
            pltpu.VMEM((tt, c), F32),
            pltpu.VMEM((tt, c), F32),
            pltpu.VMEM((tt, c), F32),
            pltpu.VMEM((1, c), F32),
        ],
        compiler_params=_params("arbitrary"),
        name="rglru_core",
    )(xz, conv_w, conv_b.reshape(1, c), w_rg.astype(BF16), b_rg.reshape(1, c),
      w_ig.astype(BF16), b_ig.reshape(1, c), lam.reshape(1, c), side_w)


def kernel(x, rel_bias, mix_norm, mlp_norm, final_norm, attn_w_qkv, attn_w_o, rec_w_in, rec_conv_w,
           rec_conv_b, rec_w_rg, rec_b_rg, rec_w_ig, rec_b_ig, rec_lambda, rec_w_out, mlp_w1, mlp_w2):
    b, s, d = x.shape
    assert b == 1, "the trunk kernels handle one sequence"
    depth = mix_norm.shape[0]
    xr = x.reshape(s, d)
    xn = _rmsnorm(xr, mix_norm[0])
    out = None
    for layer in range(depth):
        j = layer // 2
        if layer % 2 == 0:
            qkv = _proj(xn, attn_w_qkv, j, BF16, "attn_qkv_proj",
                        epilogue="scale_leading", scaled_cols=d, scale=HEAD_DIM ** -0.5 * LOG2E)
            mixed, w2 = _moba_attention(qkv, rel_bias, mlp_w2, layer)
            w_out = attn_w_o
        else:
            xz = _proj(xn, rec_w_in, j, F32, "rec_in_proj")
            mixed, w2 = _rglru(xz, rec_conv_w[j], rec_conv_b[j], rec_w_rg[j], rec_b_rg[j],
                               rec_w_ig[j], rec_b_ig[j], rec_lambda[j], mlp_w2, layer)
            w_out = rec_w_out
        xr, xn = _resid_proj(mixed, w_out, j, xr, mlp_norm[layer], "mixer_out_proj")
        hidden = _proj(xn, mlp_w1, layer, BF16, "mlp_up_proj", epilogue="relu2")
        if layer + 1 < depth:
            xr, xn = _resid_proj(hidden, w2[None], 0, xr, mix_norm[layer + 1], "mlp_down_proj")
        else:
            out = _resid_proj(hidden, w2[None], 0, xr, final_norm, "mlp_down_proj_final", final=True)
    return out.reshape(b, s, d)
```

```python
import functools
import math

import jax
import jax.numpy as jnp
from jax import lax
from jax.experimental import pallas as pl
from jax.experimental.pallas import tpu as pltpu

HEAD_DIM = 128
MOBA_BLOCK = 256
MOBA_TOPK = 3
NUM_BUCKETS = 32
MAX_DISTANCE = 4096
CONV_WIDTH = 4
LRU_C = 8.0
GATE_BLOCK = 128
EPS = 1e-6
MASKED = -1e30
LOG2E = math.log2(math.e)
ATTN_QUERY_BLOCKS = 4
ATTN_SUM_ROWS = 16

V7X_VMEM_BYTES = 64 * 1024 * 1024
V7X_LANES = 128
V7X_SUBLANES = 8
VMEM_LIMIT_BYTES = V7X_VMEM_BYTES - 8 * 1024 * 1024

F32 = jnp.float32
BF16 = jnp.bfloat16


def _tile(dim, preferred, align):
    if dim <= preferred:
        return dim
    t = (preferred // align) * align
    while t > align and dim % t:
        t -= align
    assert dim % t == 0, (dim, preferred, align)
    return t


def _params(*semantics):
    return pltpu.CompilerParams(dimension_semantics=semantics, vmem_limit_bytes=VMEM_LIMIT_BYTES)


def _rms_rows(x, g):
    y = x * lax.rsqrt(jnp.mean(x * x, axis=-1, keepdims=True) + EPS)
    return y * g


def _norm_kernel(x_ref, g_ref, o_ref):
    o_ref[...] = _rms_rows(x_ref[...], g_ref[...]).astype(o_ref.dtype)


def _rmsnorm(x, g):
    m, d = x.shape
    tm = _tile(m, 512, V7X_SUBLANES)
    return pl.pallas_call(
        _norm_kernel,
        grid=(m // tm,),
        in_specs=[pl.BlockSpec((tm, d), lambda i: (i, 0)), pl.BlockSpec((1, d), lambda i: (0, 0))],
        out_specs=pl.BlockSpec((tm, d), lambda i: (i, 0)),
        out_shape=jax.ShapeDtypeStruct((m, d), BF16),
        compiler_params=_params("arbitrary"),
        name="rmsnorm_in",
    )(x, g.reshape(1, d))


def _side_cast_specs(w, layer, n_steps, step_index):
    _, k, n = w.shape
    slab = k // n_steps
    assert slab * n_steps == k and slab % (2 * V7X_SUBLANES) == 0, (k, n_steps)
    return (pl.BlockSpec((None, slab, n), lambda *g: (layer, step_index(*g), 0)),
            pl.BlockSpec((slab, n), lambda *g: (step_index(*g), 0)),
            jax.ShapeDtypeStruct((k, n), BF16))


def _proj_kernel(a_ref, w_ref, o_ref, wb_ref, *, epilogue, scaled_tiles, scale):
    @pl.when(pl.program_id(1) == 0)
    def _():
        wb_ref[...] = w_ref[...].astype(BF16)

    acc = jnp.dot(a_ref[...], wb_ref[...], preferred_element_type=F32)
    if epilogue == "relu2":
        acc = jnp.square(jnp.maximum(acc, 0.0))
    elif epilogue == "scale_leading":
        acc = acc * jnp.where(pl.program_id(0) < scaled_tiles, scale, 1.0).astype(F32)
    o_ref[...] = acc.astype(o_ref.dtype)


def _proj(a, w, layer, out_dtype, name, epilogue="none", scaled_cols=0, scale=1.0):
    m, k = a.shape
    n = w.shape[2]
    tn = _tile(math.gcd(n, scaled_cols) if scaled_cols else n, 1024, V7X_LANES)
    fixed_bytes = k * tn * (2 * 4 + 2)
    row_bytes = 2 * k * 2 + 2 * tn * jnp.dtype(out_dtype).itemsize + 4 * tn
    tm = _tile(m, min(2048, (VMEM_LIMIT_BYTES - fixed_bytes) // row_bytes // V7X_LANES * V7X_LANES), V7X_SUBLANES)
    kern = functools.partial(_proj_kernel, epilogue=epilogue, scaled_tiles=scaled_cols // tn, scale=scale)
    return pl.pallas_call(
        kern,
        grid=(n // tn, m // tm),
        in_specs=[pl.BlockSpec((tm, k), lambda j, i: (i, 0)),
                  pl.BlockSpec((None, k, tn), lambda j, i: (layer, 0, j))],
        out_specs=pl.BlockSpec((tm, tn), lambda j, i: (i, j)),
        out_shape=jax.ShapeDtypeStruct((m, n), out_dtype),
        scratch_shapes=[pltpu.VMEM((k, tn), BF16)],
        compiler_params=_params("arbitrary", "arbitrary"),
        name=name,
    )(a, w)


def _resid_kernel(a_ref, w_ref, res_ref, g_ref, *refs, final, cast_weight):
    n_out = 1 if final else 2
    out_refs = refs[:n_out]
    if cast_weight:
        wb_ref = refs[n_out]

        @pl.when(pl.program_id(0) == 0)
        def _():
            wb_ref[...] = w_ref[...].astype(BF16)
    else:
        wb_ref = w_ref

    x = res_ref[...] + jnp.dot(a_ref[...], wb_ref[...], preferred_element_type=F32)
    y = _rms_rows(x, g_ref[...])
    if final:
        out_refs[0][...] = y
    else:
        out_refs[0][...] = x
        out_refs[1][...] = y.astype(out_refs[1].dtype)


def _resid_proj(a, w, layer, res, g, name, final=False):
    m, k = a.shape
    n = w.shape[2]
    cast_weight = w.dtype != BF16
    weight_bytes = k * n * (w.dtype.itemsize + (2 if cast_weight else 0))
    row_bytes = 2 * (2 * k + 4 * n + (4 * n if final else 6 * n)) + 4 * n
    tm = _tile(m, max(V7X_LANES, (VMEM_LIMIT_BYTES - weight_bytes) // row_bytes // V7X_LANES * V7X_LANES),
               V7X_SUBLANES)
    kern = functools.partial(_resid_kernel, final=final, cast_weight=cast_weight)
    row_spec = pl.BlockSpec((tm, n), lambda i: (i, 0))
    if final:
        out_shape = (jax.ShapeDtypeStruct((m, n), F32),)
        out_specs = (row_spec,)
    else:
        out_shape = (jax.ShapeDtypeStruct((m, n), F32), jax.ShapeDtypeStruct((m, n), BF16))
        out_specs = (row_spec, row_spec)
    outs = pl.pallas_call(
        kern,
        grid=(m // tm,),
        in_specs=[
            pl.BlockSpec((tm, k), lambda i: (i, 0)),
            pl.BlockSpec((None, k, n), lambda i: (layer, 0, 0), pipeline_mode=pl.Buffered(1)),
            row_spec,
            pl.BlockSpec((1, n), lambda i: (0, 0)),
        ],
        out_specs=out_specs,
        out_shape=out_shape,
        scratch_shapes=[pltpu.VMEM((k, n), BF16)] if cast_weight else [],
        compiler_params=_params("arbitrary"),
        name=name,
    )(a, w, res, g.reshape(1, n))
    return outs[0] if final else outs


def _attn_kernel(rb_ref, bkt_ref, q_ref, k_ref, v_ref, side_in_ref, o_ref, side_out_ref,
                 kmean_ref, kaug_ref, vt_ref, bias_ref, qaug_ref, s0_ref, s1_ref, p0_ref, p1_ref, acc_ref,
                 *, nb, qb, n_sel):
    bs = MOBA_BLOCK
    hd = HEAD_DIM
    qw = qb * bs
    h = pl.program_id(0)
    i0 = pl.program_id(1) * qb
    neg_inf = jnp.float32(-jnp.inf)
    side_out_ref[...] = side_in_ref[...].astype(BF16)

    @pl.when(pl.program_id(1) == 0)
    def _per_head_setup():
        bkt = bkt_ref[...]
        bias_row = jnp.zeros(bkt.shape, F32)
        for b in range(NUM_BUCKETS):
            bias_row = jnp.where(bkt == b, rb_ref[b, h] * LOG2E, bias_row)
        for d in range(nb):
            seg = jnp.broadcast_to(bias_row[:, d * bs:(d + 2) * bs], (bs, 2 * bs))
            tile = pltpu.roll(seg, 0, 1, stride=1, stride_axis=0)[:, bs:]
            if d == 0:
                kpos = lax.broadcasted_iota(jnp.int32, tile.shape, 0)
                qpos = lax.broadcasted_iota(jnp.int32, tile.shape, 1)
                tile = jnp.where(kpos <= qpos, tile, MASKED)
            bias_ref[d] = tile
        lane_blk = lax.broadcasted_iota(jnp.int32, (bs, V7X_LANES), 1)
        for j in range(nb):
            kj = k_ref[j * bs:(j + 1) * bs, :]
            kmean_ref[j:j + 1, :] = jnp.mean(kj.astype(F32), axis=0, keepdims=True)
            kaug_ref[j * bs:(j + 1) * bs, :] = jnp.concatenate(
                [kj, jnp.where(lane_blk == j, 1.0, 0.0).astype(BF16)], axis=1)
            ones_row = jnp.where(lax.broadcasted_iota(jnp.int32, (ATTN_SUM_ROWS, bs), 0) == 0, 1.0, 0.0)
            vt_ref[j] = jnp.concatenate([v_ref[j * bs:(j + 1) * bs, :].astype(F32).T, ones_row], axis=0).astype(BF16)

    qt = q_ref[...].astype(F32).T.astype(BF16)
    qaug_ref[:hd, :] = qt

    gate = jnp.dot(kmean_ref[...].astype(BF16), qt, preferred_element_type=F32)
    blk = lax.broadcasted_iota(jnp.int32, gate.shape, 0)
    cur = i0 + lax.broadcasted_iota(jnp.int32, gate.shape, 1) // bs
    gate = jnp.where(blk < cur, gate, neg_inf)
    madd = jnp.where(blk == cur, 0.0, MASKED).astype(F32)
    for _ in range(n_sel):
        top = jnp.max(gate, axis=0, keepdims=True)
        idx = jnp.min(jnp.where(gate == top, blk, nb), axis=0, keepdims=True)
        hit = blk == idx
        madd = jnp.where(jnp.logical_and(hit, top > neg_inf), 0.0, madd)
        gate = jnp.where(hit, neg_inf, gate)
    qaug_ref[hd:, :] = jnp.concatenate([madd, jnp.zeros((V7X_LANES - nb, qw), F32)], axis=0).astype(BF16)

    def scores(j):
        kj = kaug_ref[pl.ds(pl.multiple_of(j * bs, bs), bs), :]
        s = jnp.dot(kj, qaug_ref[...], preferred_element_type=F32)
        bias = jnp.concatenate([bias_ref[jnp.maximum(i0 + a - j, 0)] for a in range(qb)], axis=1)
        return s + bias

    last = i0 + qb - 1

    def key_block(j, m, s_cur_ref, s_next_ref, p_prev_ref, p_cur_ref):
        pv_prev = jnp.dot(vt_ref[jnp.maximum(j - 1, 0)], p_prev_ref[...], preferred_element_type=F32)
        s = s_cur_ref[...]
        m_new = jnp.maximum(m, jnp.max(s, axis=0, keepdims=True))
        p_cur_ref[...] = jnp.exp2(s - m_new).astype(BF16)
        acc_ref[...] = jnp.exp2(m - m_new) * (acc_ref[...] + pv_prev)
        s_next_ref[...] = scores(jnp.minimum(j + 1, last))
        return m_new

    def key_block_group(g, m):
        for u in range(0, qb, 2):
            m = key_block(qb * g + u, m, s0_ref, s1_ref, p1_ref, p0_ref)
            m = key_block(qb * g + u + 1, m, s1_ref, s0_ref, p0_ref, p1_ref)
        return m

    s0_ref[...] = scores(0)
    p1_ref[...] = jnp.zeros(p1_ref.shape, BF16)
    acc_ref[...] = jnp.zeros(acc_ref.shape, F32)
    lax.fori_loop(0, pl.program_id(1) + 1, key_block_group, jnp.full((1, qw), MASKED, F32))
    acc = acc_ref[...] + jnp.dot(vt_ref[last], p1_ref[...], preferred_element_type=F32)
    o_ref[...] = (acc[:hd] / acc[hd:hd + 1]).T.astype(o_ref.dtype)


def _t5_bucket(dist):
    n = jnp.maximum(dist, 0)
    max_exact = NUM_BUCKETS // 2
    nf = jnp.maximum(n, max_exact).astype(F32)
    large = max_exact + (jnp.log(nf / max_exact) / math.log(MAX_DISTANCE / max_exact)
                         * (NUM_BUCKETS - max_exact)).astype(jnp.int32)
    large = jnp.minimum(large, NUM_BUCKETS - 1)
    return jnp.where(n < max_exact, n, large)


def _moba_attention(qkv, rel_bias, side_w, side_layer, query_blocks=ATTN_QUERY_BLOCKS):
    s, three_d = qkv.shape
    d = three_d // 3
    nh = d // HEAD_DIM
    bs = MOBA_BLOCK
    assert s % bs == 0 and d % HEAD_DIM == 0
    nb = s // bs
    n_sel = min(MOBA_TOPK, nb - 1)
    qb = _tile(nb, query_blocks, 2)
    assert nb <= V7X_LANES, "the block one-hot code occupies one lane tile"
    bkt = _t5_bucket(jnp.arange(-bs, s, dtype=jnp.int32)).reshape(1, s + bs)
    kern = functools.partial(_attn_kernel, nb=nb, qb=qb, n_sel=n_sel)
    n_ib = nb // qb
    side_in, side_out, side_shape = _side_cast_specs(side_w, side_layer, nh * n_ib, lambda h, i: h * n_ib + i)
    return pl.pallas_call(
        kern,
        grid=(nh, n_ib),
        in_specs=[
            pl.BlockSpec(memory_space=pltpu.SMEM),
            pl.BlockSpec((1, s + bs), lambda h, i: (0, 0)),
            pl.BlockSpec((qb * bs, HEAD_DIM), lambda h, i: (i, h)),
            pl.BlockSpec((s, HEAD_DIM), lambda h, i: (0, nh + h)),
            pl.BlockSpec((s, HEAD_DIM), lambda h, i: (0, 2 * nh + h)),
            side_in,
        ],
        out_specs=(pl.BlockSpec((qb * bs, HEAD_DIM), lambda h, i: (i, h)), side_out),
        out_shape=(jax.ShapeDtypeStruct((s, d), BF16), side_shape),
        scratch_shapes=[
            pltpu.VMEM((nb, HEAD_DIM), F32),
            pltpu.VMEM((s, HEAD_DIM + V7X_LANES), BF16),
            pltpu.VMEM((nb, HEAD_DIM + ATTN_SUM_ROWS, bs), BF16),
            pltpu.VMEM((nb, bs, bs), F32),
            pltpu.VMEM((HEAD_DIM + V7X_LANES, qb * bs), BF16),
            pltpu.VMEM((bs, qb * bs), F32),
            pltpu.VMEM((bs, qb * bs), F32),
            pltpu.VMEM((bs, qb * bs), BF16),
            pltpu.VMEM((bs, qb * bs), BF16),
            pltpu.VMEM((HEAD_DIM + ATTN_SUM_ROWS, qb * bs), F32),
        ],
        compiler_params=_params("arbitrary", "arbitrary"),
        name="moba_attention",
    )(rel_bias, bkt, qkv, qkv, qkv, side_w)


def _sigmoid(x):
    return 0.5 * jnp.tanh(0.5 * x) + 0.5


def _rglru_kernel(xz_ref, cw_ref, cb_ref, wrg_ref, brg_ref, wig_ref, big_ref, lam_ref, side_in_ref,
                  y_ref, side_out_ref, xbuf_ref, xc_ref, a_ref, b_ref, h_ref, *, tt, c):
    pad = V7X_SUBLANES
    side_out_ref[...] = side_in_ref[...].astype(BF16)

    @pl.when(pl.program_id(0) == 0)
    def _():
        xbuf_ref[0:pad, :] = jnp.zeros((pad, c), F32)
        h_ref[...] = jnp.zeros((1, c), F32)

    xbuf_ref[pad:pad + tt, :] = xz_ref[:, :c]
    xc = cb_ref[...]
    for j in range(CONV_WIDTH):
        lag = CONV_WIDTH - 1 - j
        xc = xc + xbuf_ref[pad - lag:pad - lag + tt, :] * cw_ref[j:j + 1, :]
    xc_ref[...] = xc
    xbuf_ref[0:pad, :] = xbuf_ref[tt:tt + pad, :]

    for n in range(c // GATE_BLOCK):
        sl = slice(n * GATE_BLOCK, (n + 1) * GATE_BLOCK)
        xcn = xc_ref[:, sl]
        xb = xcn.astype(BF16)
        r = _sigmoid(jnp.dot(xb, wrg_ref[n], preferred_element_type=F32) + brg_ref[:, sl])
        ig = _sigmoid(jnp.dot(xb, wig_ref[n], preferred_element_type=F32) + big_ref[:, sl])
        log_a = (LRU_C * r) * jax.nn.log_sigmoid(lam_ref[:, sl])
        a = jnp.exp(log_a)
        a_ref[:, sl] = a
        w = -jnp.tanh(log_a) * (1.0 + a * a)
        b_ref[:, sl] = jnp.where(w > 0.0, w * lax.rsqrt(w), 0.0) * (ig * xcn)

    def step(t, hprev):
        hnew = a_ref[pl.ds(t, 1), :] * hprev + b_ref[pl.ds(t, 1), :]
        b_ref[pl.ds(t, 1), :] = hnew
        return hnew

    h_ref[...] = lax.fori_loop(0, tt, step, h_ref[...], unroll=8)
    y_ref[...] = (b_ref[...] * jax.nn.gelu(xz_ref[:, c:])).astype(y_ref.dtype)


def _rglru(xz, conv_w, conv_b, w_rg, b_rg, w_ig, b_ig, lam, side_w, side_layer):
    s, two_c = xz.shape
    c = two_c // 2
    assert c % GATE_BLOCK == 0
    ngb = c // GATE_BLOCK
    tt = _tile(s, 256, V7X_SUBLANES)
    kern = functools.partial(_rglru_kernel, tt=tt, c=c)
    vec = pl.BlockSpec((1, c), lambda t: (0, 0))
    gate_w = pl.BlockSpec((ngb, GATE_BLOCK, GATE_BLOCK), lambda t: (0, 0, 0))
    side_in, side_out, side_shape = _side_cast_specs(side_w, side_layer, s // tt, lambda t: t)
    return pl.pallas_call(
        kern,
        grid=(s // tt,),
        in_specs=[
            pl.BlockSpec((tt, two_c), lambda t: (t, 0)),
            pl.BlockSpec((CONV_WIDTH, c), lambda t: (0, 0)),
            vec, gate_w, vec, gate_w, vec, vec, side_in,
        ],
        out_specs=(pl.BlockSpec((tt, c), lambda t: (t, 0)), side_out),
        out_shape=(jax.ShapeDtypeStruct((s, c), BF16), side_shape),
        scratch_shapes=[
            pltpu.VMEM((tt + 2 * V7X_SUBLANES, c), F32),
            pltpu.VMEM((tt, c), F32),
            pltpu.VMEM((tt, c), F32),
            pltpu.VMEM((tt, c), F32),
            pltpu.VMEM((1, c), F32),
        ],
        compiler_params=_params("arbitrary"),
        name="rglru_core",
    )(xz, conv_w, conv_b.reshape(1, c), w_rg.astype(BF16), b_rg.reshape(1, c),
      w_ig.astype(BF16), b_ig.reshape(1, c), lam.reshape(1, c), side_w)


def kernel(x, rel_bias, mix_norm, mlp_norm, final_norm, attn_w_qkv, attn_w_o, rec_w_in, rec_conv_w,
           rec_conv_b, rec_w_rg, rec_b_rg, rec_w_ig, rec_b_ig, rec_lambda, rec_w_out, mlp_w1, mlp_w2):
    b, s, d = x.shape
    assert b == 1, "the trunk kernels handle one sequence"
    depth = mix_norm.shape[0]
    xr = x.reshape(s, d)
    xn = _rmsnorm(xr, mix_norm[0])
    out = None
    for layer in range(depth):
        j = layer // 2
        if layer % 2 == 0:
            qkv = _proj(xn, attn_w_qkv, j, BF16, "attn_qkv_proj",
                        epilogue="scale_leading", scaled_cols=d, scale=HEAD_DIM ** -0.5 * LOG2E)
            mixed, w2 = _moba_attention(qkv, rel_bias, mlp_w2, layer, query_blocks=4 if layer == 0 else 2)
            w_out = attn_w_o
        else:
            xz = _proj(xn, rec_w_in, j, F32, "rec_in_proj")
            mixed, w2 = _rglru(xz, rec_conv_w[j], rec_conv_b[j], rec_w_rg[j], rec_b_rg[j],
                               rec_w_ig[j], rec_b_ig[j], rec_lambda[j], mlp_w2, layer)
            w_out = rec_w_out
        xr, xn = _resid_proj(mixed, w_out, j, xr, mlp_norm[layer], "mixer_out_proj")
        hidden = _proj(xn, mlp_w1, layer, BF16, "mlp_up_proj", epilogue="relu2")
        if layer + 1 < depth:
            xr, xn = _resid_proj(hidden, w2[None], 0, xr, mix_norm[layer + 1], "mlp_down_proj")
        else:
            out = _resid_proj(hidden, w2[None], 0, xr, final_norm, "mlp_down_proj_final", final=True)
    return out.reshape(b, s, d)
```

```python
import functools
import math

import jax
import jax.numpy as jnp
from jax import lax
from jax.experimental import pallas as pl
from jax.experimental.pallas import tpu as pltpu

HEAD_DIM = 128
MOBA_BLOCK = 256
MOBA_TOPK = 3
NUM_BUCKETS = 32
MAX_DISTANCE = 4096
CONV_WIDTH = 4
LRU_C = 8.0
GATE_BLOCK = 128
EPS = 1e-6
MASKED = -1e30
LOG2E = math.log2(math.e)
ATTN_QUERY_BLOCKS = 4
ATTN_SUM_ROWS = 16

V7X_VMEM_BYTES = 64 * 1024 * 1024
V7X_LANES = 128
V7X_SUBLANES = 8
VMEM_LIMIT_BYTES = V7X_VMEM_BYTES - 8 * 1024 * 1024

F32 = jnp.float32
BF16 = jnp.bfloat16


def _tile(dim, preferred, align):
    if dim <= preferred:
        return dim
    t = (preferred // align) * align
    while t > align and dim % t:
        t -= align
    assert dim % t == 0, (dim, preferred, align)
    return t


def _params(*semantics):
    return pltpu.CompilerParams(dimension_semantics=semantics, vmem_limit_bytes=VMEM_LIMIT_BYTES)


def _rms_rows(x, g):
    y = x * lax.rsqrt(jnp.mean(x * x, axis=-1, keepdims=True) + EPS)
    return y * g


def _norm_kernel(x_ref, g_ref, o_ref):
    o_ref[...] = _rms_rows(x_ref[...], g_ref[...]).astype(o_ref.dtype)


def _rmsnorm(x, g):
    m, d = x.shape
    tm = _tile(m, 512, V7X_SUBLANES)
    return pl.pallas_call(
        _norm_kernel,
        grid=(m // tm,),
        in_specs=[pl.BlockSpec((tm, d), lambda i: (i, 0)), pl.BlockSpec((1, d), lambda i: (0, 0))],
        out_specs=pl.BlockSpec((tm, d), lambda i: (i, 0)),
        out_shape=jax.ShapeDtypeStruct((m, d), BF16),
        compiler_params=_params("arbitrary"),
        name="rmsnorm_in",
    )(x, g.reshape(1, d))


def _side_cast_specs(w, layer, n_steps, step_index):
    _, k, n = w.shape
    slab = k // n_steps
    assert slab * n_steps == k and slab % (2 * V7X_SUBLANES) == 0, (k, n_steps)
    return (pl.BlockSpec((None, slab, n), lambda *g: (layer, step_index(*g), 0)),
            pl.BlockSpec((slab, n), lambda *g: (step_index(*g), 0)),
            jax.ShapeDtypeStruct((k, n), BF16))


def _proj_kernel(a_ref, w_ref, o_ref, wb_ref, *, epilogue, scaled_tiles, scale):
    @pl.when(pl.program_id(1) == 0)
    def _():
        wb_ref[...] = w_ref[...].astype(BF16)

    acc = jnp.dot(a_ref[...], wb_ref[...], preferred_element_type=F32)
    if epilogue == "relu2":
        acc = jnp.square(jnp.maximum(acc, 0.0))
    elif epilogue == "scale_leading":
        acc = acc * jnp.where(pl.program_id(0) < scaled_tiles, scale, 1.0).astype(F32)
    o_ref[...] = acc.astype(o_ref.dtype)


def _proj(a, w, layer, out_dtype, name, epilogue="none", scaled_cols=0, scale=1.0):
    m, k = a.shape
    n = w.shape[2]
    tn = _tile(math.gcd(n, scaled_cols) if scaled_cols else n, 1024, V7X_LANES)
    fixed_bytes = k * tn * (2 * 4 + 2)
    row_bytes = 2 * k * 2 + 2 * tn * jnp.dtype(out_dtype).itemsize + 4 * tn
    tm = _tile(m, min(2048, (VMEM_LIMIT_BYTES - fixed_bytes) // row_bytes // V7X_LANES * V7X_LANES), V7X_SUBLANES)
    kern = functools.partial(_proj_kernel, epilogue=epilogue, scaled_tiles=scaled_cols // tn, scale=scale)
    return pl.pallas_call(
        kern,
        grid=(n // tn, m // tm),
        in_specs=[pl.BlockSpec((tm, k), lambda j, i: (i, 0)),
                  pl.BlockSpec((None, k, tn), lambda j, i: (layer, 0, j))],
        out_specs=pl.BlockSpec((tm, tn), lambda j, i: (i, j)),
        out_shape=jax.ShapeDtypeStruct((m, n), out_dtype),
        scratch_shapes=[pltpu.VMEM((k, tn), BF16)],
        compiler_params=_params("arbitrary", "arbitrary"),
        name=name,
    )(a, w)


def _resid_kernel(a_ref, w_ref, res_ref, g_ref, *refs, final, cast_weight):
    n_out = 1 if final else 2
    out_refs = refs[:n_out]
    if cast_weight:
        wb_ref = refs[n_out]

        @pl.when(pl.program_id(0) == 0)
        def _():
            wb_ref[...] = w_ref[...].astype(BF16)
    else:
        wb_ref = w_ref

    x = res_ref[...] + jnp.dot(a_ref[...], wb_ref[...], preferred_element_type=F32)
    y = _rms_rows(x, g_ref[...])
    if final:
        out_refs[0][...] = y
    else:
        out_refs[0][...] = x
        out_refs[1][...] = y.astype(out_refs[1].dtype)


def _resid_proj(a, w, layer, res, g, name, final=False):
    m, k = a.shape
    n = w.shape[2]
    cast_weight = w.dtype != BF16
    weight_bytes = k * n * (w.dtype.itemsize + (2 if cast_weight else 0))
    row_bytes = 2 * (2 * k + 4 * n + (4 * n if final else 6 * n)) + 4 * n
    tm = _tile(m, max(V7X_LANES, (VMEM_LIMIT_BYTES - weight_bytes) // row_bytes // V7X_LANES * V7X_LANES),
               V7X_SUBLANES)
    kern = functools.partial(_resid_kernel, final=final, cast_weight=cast_weight)
    row_spec = pl.BlockSpec((tm, n), lambda i: (i, 0))
    if final:
        out_shape = (jax.ShapeDtypeStruct((m, n), F32),)
        out_specs = (row_spec,)
    else:
        out_shape = (jax.ShapeDtypeStruct((m, n), F32), jax.ShapeDtypeStruct((m, n), BF16))
        out_specs = (row_spec, row_spec)
    outs = pl.pallas_call(
        kern,
        grid=(m // tm,),
        in_specs=[
            pl.BlockSpec((tm, k), lambda i: (i, 0)),
            pl.BlockSpec((None, k, n), lambda i: (layer, 0, 0), pipeline_mode=pl.Buffered(1)),
            row_spec,
            pl.BlockSpec((1, n), lambda i: (0, 0)),
        ],
        out_specs=out_specs,
        out_shape=out_shape,
        scratch_shapes=[pltpu.VMEM((k, n), BF16)] if cast_weight else [],
        compiler_params=_params("arbitrary"),
        name=name,
    )(a, w, res, g.reshape(1, n))
    return outs[0] if final else outs


def _attn_kernel(rb_ref, bkt_ref, q_ref, k_ref, v_ref, side_in_ref, o_ref, side_out_ref,
                 kmean_ref, kaug_ref, vt_ref, bias_ref, qaug_ref, s0_ref, s1_ref, p0_ref, p1_ref, acc_ref,
                 *, nb, qb, n_sel):
    bs = MOBA_BLOCK
    hd = HEAD_DIM
    qw = qb * bs
    h = pl.program_id(0)
    i0 = pl.program_id(1) * qb
    neg_inf = jnp.float32(-jnp.inf)
    side_out_ref[...] = side_in_ref[...].astype(BF16)

    @pl.when(pl.program_id(1) == 0)
    def _per_head_setup():
        bkt = bkt_ref[...]
        bias_row = jnp.zeros(bkt.shape, F32)
        for b in range(NUM_BUCKETS):
            bias_row = jnp.where(bkt == b, rb_ref[b, h] * LOG2E, bias_row)
        for d in range(nb):
            seg = jnp.broadcast_to(bias_row[:, d * bs:(d + 2) * bs], (bs, 2 * bs))
            tile = pltpu.roll(seg, 0, 1, stride=1, stride_axis=0)[:, bs:]
            if d == 0:
                kpos = lax.broadcasted_iota(jnp.int32, tile.shape, 0)
                qpos = lax.broadcasted_iota(jnp.int32, tile.shape, 1)
                tile = jnp.where(kpos <= qpos, tile, MASKED)
            bias_ref[d] = tile
        lane_blk = lax.broadcasted_iota(jnp.int32, (bs, V7X_LANES), 1)
        for j in range(nb):
            kj = k_ref[j * bs:(j + 1) * bs, :]
            kmean_ref[j:j + 1, :] = jnp.mean(kj.astype(F32), axis=0, keepdims=True)
            kaug_ref[j * bs:(j + 1) * bs, :] = jnp.concatenate(
                [kj, jnp.where(lane_blk == j, 1.0, 0.0).astype(BF16)], axis=1)
            ones_row = jnp.where(lax.broadcasted_iota(jnp.int32, (ATTN_SUM_ROWS, bs), 0) == 0, 1.0, 0.0)
            vt_ref[j] = jnp.concatenate([v_ref[j * bs:(j + 1) * bs, :].astype(F32).T, ones_row], axis=0).astype(BF16)

    qt = q_ref[...].astype(F32).T.astype(BF16)
    qaug_ref[:hd, :] = qt

    gate = jnp.dot(kmean_ref[...].astype(BF16), qt, preferred_element_type=F32)
    blk = lax.broadcasted_iota(jnp.int32, gate.shape, 0)
    cur = i0 + lax.broadcasted_iota(jnp.int32, gate.shape, 1) // bs
    gate = jnp.where(blk < cur, gate, neg_inf)
    madd = jnp.where(blk == cur, 0.0, MASKED).astype(F32)
    for _ in range(n_sel):
        top = jnp.max(gate, axis=0, keepdims=True)
        idx = jnp.min(jnp.where(gate == top, blk, nb), axis=0, keepdims=True)
        hit = blk == idx
        madd = jnp.where(jnp.logical_and(hit, top > neg_inf), 0.0, madd)
        gate = jnp.where(hit, neg_inf, gate)
    qaug_ref[hd:, :] = jnp.concatenate([madd, jnp.zeros((V7X_LANES - nb, qw), F32)], axis=0).astype(BF16)

    def scores(j, lo):
        kj = kaug_ref[pl.ds(pl.multiple_of(j * bs, bs), bs), :]
        s = jnp.dot(kj, qaug_ref[:, lo:], preferred_element_type=F32)
        bias = jnp.concatenate([bias_ref[jnp.maximum(i0 + a - j, 0)] for a in range(lo // bs, qb)], axis=1)
        return s + bias


    def own_block(u, m, s_cur_ref, s_next_ref, p_prev_ref, p_cur_ref):
        lo, lo_prev = u * bs, max(u - 1, 0) * bs
        pv_prev = jnp.dot(vt_ref[jnp.maximum(i0 + u - 1, 0)], p_prev_ref[:, lo_prev:], preferred_element_type=F32)
        s = s_cur_ref[:, lo:]
        m_new = jnp.maximum(m[:, lo:], jnp.max(s, axis=0, keepdims=True))
        p_cur_ref[:, lo:] = jnp.exp2(s - m_new).astype(BF16)
        if lo_prev < lo:
            acc_ref[:, lo_prev:lo] = acc_ref[:, lo_prev:lo] + pv_prev[:, :lo - lo_prev]
        acc_ref[:, lo:] = jnp.exp2(m[:, lo:] - m_new) * (acc_ref[:, lo:] + pv_prev[:, lo - lo_prev:])
        if u + 1 < qb:
            s_next_ref[:, lo + bs:] = scores(i0 + u + 1, lo + bs)
        else:
            s_next_ref[...] = scores(0, 0)
        return m_new if lo == 0 else jnp.concatenate([m[:, :lo], m_new], axis=1)

    def past_block(j, m, s_cur_ref, s_next_ref, p_prev_ref, p_cur_ref):
        pv_prev = jnp.dot(vt_ref[jnp.maximum(j - 1, 0)], p_prev_ref[...], preferred_element_type=F32)
        s = s_cur_ref[...]
        m_new = jnp.maximum(m, jnp.max(s, axis=0, keepdims=True))
        p_cur_ref[...] = jnp.exp2(s - m_new).astype(BF16)
        acc_ref[...] = jnp.exp2(m - m_new) * (acc_ref[...] + pv_prev)
        s_next_ref[...] = scores(jnp.minimum(j + 1, i0 - 1), 0)
        return m_new

    def past_group(g, m):
        for u in range(0, qb, 2):
            m = past_block(qb * g + u, m, s0_ref, s1_ref, p1_ref, p0_ref)
            m = past_block(qb * g + u + 1, m, s1_ref, s0_ref, p0_ref, p1_ref)
        return m

    s0_ref[...] = scores(i0, 0)
    p1_ref[...] = jnp.zeros(p1_ref.shape, BF16)
    acc_ref[...] = jnp.zeros(acc_ref.shape, F32)
    m = jnp.full((1, qw), MASKED, F32)
    for u in range(0, qb, 2):
        m = own_block(u, m, s0_ref, s1_ref, p1_ref, p0_ref)
        m = own_block(u + 1, m, s1_ref, s0_ref, p0_ref, p1_ref)
    lo = (qb - 1) * bs
    acc_ref[:, lo:] = acc_ref[:, lo:] + jnp.dot(vt_ref[i0 + qb - 1], p1_ref[:, lo:], preferred_element_type=F32)
    p1_ref[...] = jnp.zeros(p1_ref.shape, BF16)
    lax.fori_loop(0, pl.program_id(1), past_group, m)
    acc = acc_ref[...] + jnp.dot(vt_ref[jnp.maximum(i0 - 1, 0)], p1_ref[...], preferred_element_type=F32)
    o_ref[...] = (acc[:hd] / acc[hd:hd + 1]).T.astype(o_ref.dtype)


def _t5_bucket(dist):
    n = jnp.maximum(dist, 0)
    max_exact = NUM_BUCKETS // 2
    nf = jnp.maximum(n, max_exact).astype(F32)
    large = max_exact + (jnp.log(nf / max_exact) / math.log(MAX_DISTANCE / max_exact)
                         * (NUM_BUCKETS - max_exact)).astype(jnp.int32)
    large = jnp.minimum(large, NUM_BUCKETS - 1)
    return jnp.where(n < max_exact, n, large)


def _moba_attention(qkv, rel_bias, side_w, side_layer):
    s, three_d = qkv.shape
    d = three_d // 3
    nh = d // HEAD_DIM
    bs = MOBA_BLOCK
    assert s % bs == 0 and d % HEAD_DIM == 0
    nb = s // bs
    n_sel = min(MOBA_TOPK, nb - 1)
    qb = _tile(nb, ATTN_QUERY_BLOCKS, 2)
    assert nb <= V7X_LANES, "the block one-hot code occupies one lane tile"
    bkt = _t5_bucket(jnp.arange(-bs, s, dtype=jnp.int32)).reshape(1, s + bs)
    kern = functools.partial(_attn_kernel, nb=nb, qb=qb, n_sel=n_sel)
    n_ib = nb // qb
    side_in, side_out, side_shape = _side_cast_specs(side_w, side_layer, nh * n_ib, lambda h, i: h * n_ib + i)
    return pl.pallas_call(
        kern,
        grid=(nh, n_ib),
        in_specs=[
            pl.BlockSpec(memory_space=pltpu.SMEM),
            pl.BlockSpec((1, s + bs), lambda h, i: (0, 0)),
            pl.BlockSpec((qb * bs, HEAD_DIM), lambda h, i: (i, h)),
            pl.BlockSpec((s, HEAD_DIM), lambda h, i: (0, nh + h)),
            pl.BlockSpec((s, HEAD_DIM), lambda h, i: (0, 2 * nh + h)),
            side_in,
        ],
        out_specs=(pl.BlockSpec((qb * bs, HEAD_DIM), lambda h, i: (i, h)), side_out),
        out_shape=(jax.ShapeDtypeStruct((s, d), BF16), side_shape),
        scratch_shapes=[
            pltpu.VMEM((nb, HEAD_DIM), F32),
            pltpu.VMEM((s, HEAD_DIM + V7X_LANES), BF16),
            pltpu.VMEM((nb, HEAD_DIM + ATTN_SUM_ROWS, bs), BF16),
            pltpu.VMEM((nb, bs, bs), F32),
            pltpu.VMEM((HEAD_DIM + V7X_LANES, qb * bs), BF16),
            pltpu.VMEM((bs, qb * bs), F32),
            pltpu.VMEM((bs, qb * bs), F32),
            pltpu.VMEM((bs, qb * bs), BF16),
            pltpu.VMEM((bs, qb * bs), BF16),
            pltpu.VMEM((HEAD_DIM + ATTN_SUM_ROWS, qb * bs), F32),
        ],
        compiler_params=_params("arbitrary", "arbitrary"),
        name="moba_attention",
    )(rel_bias, bkt, qkv, qkv, qkv, side_w)


def _sigmoid(x):
    return 0.5 * jnp.tanh(0.5 * x) + 0.5


def _rglru_kernel(xz_ref, cw_ref, cb_ref, wrg_ref, brg_ref, wig_ref, big_ref, lam_ref, side_in_ref,
                  y_ref, side_out_ref, xbuf_ref, xc_ref, a_ref, b_ref, h_ref, *, tt, c):
    pad = V7X_SUBLANES
    side_out_ref[...] = side_in_ref[...].astype(BF16)

    @pl.when(pl.program_id(0) == 0)
    def _():
        xbuf_ref[0:pad, :] = jnp.zeros((pad, c), F32)
        h_ref[...] = jnp.zeros((1, c), F32)

    xbuf_ref[pad:pad + tt, :] = xz_ref[:, :c]
    xc = cb_ref[...]
    for j in range(CONV_WIDTH):
        lag = CONV_WIDTH - 1 - j
        xc = xc + xbuf_ref[pad - lag:pad - lag + tt, :] * cw_ref[j:j + 1, :]
    xc_ref[...] = xc
    xbuf_ref[0:pad, :] = xbuf_ref[tt:tt + pad, :]

    for n in range(c // GATE_BLOCK):
        sl = slice(n * GATE_BLOCK, (n + 1) * GATE_BLOCK)
        xcn = xc_ref[:, sl]
        xb = xcn.astype(BF16)
        r = _sigmoid(jnp.dot(xb, wrg_ref[n], preferred_element_type=F32) + brg_ref[:, sl])
        ig = _sigmoid(jnp.dot(xb, wig_ref[n], preferred_element_type=F32) + big_ref[:, sl])
        log_a = (LRU_C * r) * jax.nn.log_sigmoid(lam_ref[:, sl])
        a = jnp.exp(log_a)
        a_ref[:, sl] = a
        w = -jnp.tanh(log_a) * (1.0 + a * a)
        b_ref[:, sl] = jnp.where(w > 0.0, w * lax.rsqrt(w), 0.0) * (ig * xcn)

    def step(t, hprev):
        hnew = a_ref[pl.ds(t, 1), :] * hprev + b_ref[pl.ds(t, 1), :]
        b_ref[pl.ds(t, 1), :] = hnew
        return hnew

    h_ref[...] = lax.fori_loop(0, tt, step, h_ref[...], unroll=8)
    y_ref[...] = (b_ref[...] * jax.nn.gelu(xz_ref[:, c:])).astype(y_ref.dtype)


def _rglru(xz, conv_w, conv_b, w_rg, b_rg, w_ig, b_ig, lam, side_w, side_layer):
    s, two_c = xz.shape
    c = two_c // 2
    assert c % GATE_BLOCK == 0
    ngb = c // GATE_BLOCK
    tt = _tile(s, 256, V7X_SUBLANES)
    kern = functools.partial(_rglru_kernel, tt=tt, c=c)
    vec = pl.BlockSpec((1, c), lambda t: (0, 0))
    gate_w = pl.BlockSpec((ngb, GATE_BLOCK, GATE_BLOCK), lambda t: (0, 0, 0))
    side_in, side_out, side_shape = _side_cast_specs(side_w, side_layer, s // tt, lambda t: t)
    return pl.pallas_call(
        kern,
        grid=(s // tt,),
        in_specs=[
            pl.BlockSpec((tt, two_c), lambda t: (t, 0)),
            pl.BlockSpec((CONV_WIDTH, c), lambda t: (0, 0)),
            vec, gate_w, vec, gate_w, vec, vec, side_in,
        ],
        out_specs=(pl.BlockSpec((tt, c), lambda t: (t, 0)), side_out),
        out_shape=(jax.ShapeDtypeStruct((s, c), BF16), side_shape),
        scratch_shapes=[
            pltpu.VMEM((tt + 2 * V7X_SUBLANES, c), F32),
            pltpu.VMEM((tt, c), F32),
            pltpu.VMEM((tt, c), F32),
            pltpu.VMEM((tt, c), F32),
            pltpu.VMEM((1, c), F32),
        ],
        compiler_params=_params("arbitrary"),
        name="rglru_core",
    )(xz, conv_w, conv_b.reshape(1, c), w_rg.astype(BF16), b_rg.reshape(1, c),
      w_ig.astype(BF16), b_ig.reshape(1, c), lam.reshape(1, c), side_w)


def kernel(x, rel_bias, mix_norm, mlp_norm, final_norm, attn_w_qkv, attn_w_o, rec_w_in, rec_conv_w,
           rec_conv_b, rec_w_rg, rec_b_rg, rec_w_ig, rec_b_ig, rec_lambda, rec_w_out, mlp_w1, mlp_w2):
    b, s, d = x.shape
    assert b == 1, "the trunk kernels handle one sequence"
    depth = mix_norm.shape[0]
    xr = x.reshape(s, d)
    xn = _rmsnorm(xr, mix_norm[0])
    out = None
    for layer in range(depth):
        j = layer // 2
        if layer % 2 == 0:
            qkv = _proj(xn, attn_w_qkv, j, BF16, "attn_qkv_proj",
                        epilogue="scale_leading", scaled_cols=d, scale=HEAD_DIM ** -0.5 * LOG2E)
            mixed, w2 = _moba_attention(qkv, rel_bias, mlp_w2, layer)
            w_out = attn_w_o
        else:
            xz = _proj(xn, rec_w_in, j, F32, "rec_in_proj")
            mixed, w2 = _rglru(xz, rec_conv_w[j], rec_conv_b[j], rec_w_rg[j], rec_b_rg[j],
                               rec_w_ig[j], rec_b_ig[j], rec_lambda[j], mlp_w2, layer)
            w_out = rec_w_out
        xr, xn = _resid_proj(mixed, w_out, j, xr, mlp_norm[layer], "mixer_out_proj")
        hidden = _proj(xn, mlp_w1, layer, BF16, "mlp_up_proj", epilogue="relu2")
        if layer + 1 < depth:
            xr, xn = _resid_proj(hidden, w2[None], 0, xr, mix_norm[layer + 1], "mlp_down_proj")
        else:
            out = _resid_proj(hidden, w2[None], 0, xr, final_norm, "mlp_down_proj_final", final=True)
    return out.reshape(b, s, d)
```

```python
import functools
import math

import jax
import jax.numpy as jnp
from jax import lax
from jax.experimental import pallas as pl
from jax.experimental.pallas import tpu as pltpu

HEAD_DIM = 128
MOBA_BLOCK = 256
MOBA_TOPK = 3
NUM_BUCKETS = 32
MAX_DISTANCE = 4096
CONV_WIDTH = 4
LRU_C = 8.0
GATE_BLOCK = 128
EPS = 1e-6
MASKED = -1e30
LOG2E = math.log2(math.e)
ATTN_QUERY_BLOCKS = 4
ATTN_SUM_ROWS = 16

V7X_VMEM_BYTES = 64 * 1024 * 1024
V7X_LANES = 128
V7X_SUBLANES = 8
VMEM_LIMIT_BYTES = V7X_VMEM_BYTES - 8 * 1024 * 1024

F32 = jnp.float32
BF16 = jnp.bfloat16


def _tile(dim, preferred, align):
    if dim <= preferred:
        return dim
    t = (preferred // align) * align
    while t > align and dim % t:
        t -= align
    assert dim % t == 0, (dim, preferred, align)
    return t


def _params(*semantics):
    return pltpu.CompilerParams(dimension_semantics=semantics, vmem_limit_bytes=VMEM_LIMIT_BYTES)


def _rms_rows(x, g):
    y = x * lax.rsqrt(jnp.mean(x * x, axis=-1, keepdims=True) + EPS)
    return y * g


def _norm_kernel(x_ref, g_ref, o_ref):
    o_ref[...] = _rms_rows(x_ref[...], g_ref[...]).astype(o_ref.dtype)


def _rmsnorm(x, g):
    m, d = x.shape
    tm = _tile(m, 512, V7X_SUBLANES)
    return pl.pallas_call(
        _norm_kernel,
        grid=(m // tm,),
        in_specs=[pl.BlockSpec((tm, d), lambda i: (i, 0)), pl.BlockSpec((1, d), lambda i: (0, 0))],
        out_specs=pl.BlockSpec((tm, d), lambda i: (i, 0)),
        out_shape=jax.ShapeDtypeStruct((m, d), BF16),
        compiler_params=_params("arbitrary"),
        name="rmsnorm_in",
    )(x, g.reshape(1, d))


def _side_cast_specs(w, layer, n_steps, step_index):
    _, k, n = w.shape
    slab = k // n_steps
    assert slab * n_steps == k and slab % (2 * V7X_SUBLANES) == 0, (k, n_steps)
    return (pl.BlockSpec((None, slab, n), lambda *g: (layer, step_index(*g), 0)),
            pl.BlockSpec((slab, n), lambda *g: (step_index(*g), 0)),
            jax.ShapeDtypeStruct((k, n), BF16))


def _proj_kernel(a_ref, w_ref, o_ref, wb_ref, *, epilogue, scaled_tiles, scale):
    @pl.when(pl.program_id(1) == 0)
    def _():
        wb_ref[...] = w_ref[...].astype(BF16)

    acc = jnp.dot(a_ref[...], wb_ref[...], preferred_element_type=F32)
    if epilogue == "relu2":
        acc = jnp.square(jnp.maximum(acc, 0.0))
    elif epilogue == "scale_leading":
        acc = acc * jnp.where(pl.program_id(0) < scaled_tiles, scale, 1.0).astype(F32)
    o_ref[...] = acc.astype(o_ref.dtype)


def _proj(a, w, layer, out_dtype, name, epilogue="none", scaled_cols=0, scale=1.0):
    m, k = a.shape
    n = w.shape[2]
    tn = _tile(math.gcd(n, scaled_cols) if scaled_cols else n, 1024, V7X_LANES)
    fixed_bytes = k * tn * (2 * 4 + 2)
    row_bytes = 2 * k * 2 + 2 * tn * jnp.dtype(out_dtype).itemsize + 4 * tn
    tm = _tile(m, min(2048, (VMEM_LIMIT_BYTES - fixed_bytes) // row_bytes // V7X_LANES * V7X_LANES), V7X_SUBLANES)
    kern = functools.partial(_proj_kernel, epilogue=epilogue, scaled_tiles=scaled_cols // tn, scale=scale)
    return pl.pallas_call(
        kern,
        grid=(n // tn, m // tm),
        in_specs=[pl.BlockSpec((tm, k), lambda j, i: (i, 0)),
                  pl.BlockSpec((None, k, tn), lambda j, i: (layer, 0, j))],
        out_specs=pl.BlockSpec((tm, tn), lambda j, i: (i, j)),
        out_shape=jax.ShapeDtypeStruct((m, n), out_dtype),
        scratch_shapes=[pltpu.VMEM((k, tn), BF16)],
        compiler_params=_params("arbitrary", "arbitrary"),
        name=name,
    )(a, w)


def _resid_kernel(a_ref, w_ref, res_ref, g_ref, *refs, final, cast_weight):
    n_out = 1 if final else 2
    out_refs = refs[:n_out]
    if cast_weight:
        wb_ref = refs[n_out]

        @pl.when(pl.program_id(0) == 0)
        def _():
            wb_ref[...] = w_ref[...].astype(BF16)
    else:
        wb_ref = w_ref

    x = res_ref[...] + jnp.dot(a_ref[...], wb_ref[...], preferred_element_type=F32)
    y = _rms_rows(x, g_ref[...])
    if final:
        out_refs[0][...] = y
    else:
        out_refs[0][...] = x
        out_refs[1][...] = y.astype(out_refs[1].dtype)


def _resid_proj(a, w, layer, res, g, name, final=False):
    m, k = a.shape
    n = w.shape[2]
    cast_weight = w.dtype != BF16
    weight_bytes = k * n * (w.dtype.itemsize + (2 if cast_weight else 0))
    row_bytes = 2 * (2 * k + 4 * n + (4 * n if final else 6 * n)) + 4 * n
    tm = _tile(m, max(V7X_LANES, (VMEM_LIMIT_BYTES - weight_bytes) // row_bytes // V7X_LANES * V7X_LANES),
               V7X_SUBLANES)
    kern = functools.partial(_resid_kernel, final=final, cast_weight=cast_weight)
    row_spec = pl.BlockSpec((tm, n), lambda i: (i, 0))
    if final:
        out_shape = (jax.ShapeDtypeStruct((m, n), F32),)
        out_specs = (row_spec,)
    else:
        out_shape = (jax.ShapeDtypeStruct((m, n), F32), jax.ShapeDtypeStruct((m, n), BF16))
        out_specs = (row_spec, row_spec)
    outs = pl.pallas_call(
        kern,
        grid=(m // tm,),
        in_specs=[
            pl.BlockSpec((tm, k), lambda i: (i, 0)),
            pl.BlockSpec((None, k, n), lambda i: (layer, 0, 0), pipeline_mode=pl.Buffered(1)),
            row_spec,
            pl.BlockSpec((1, n), lambda i: (0, 0)),
        ],
        out_specs=out_specs,
        out_shape=out_shape,
        scratch_shapes=[pltpu.VMEM((k, n), BF16)] if cast_weight else [],
        compiler_params=_params("arbitrary"),
        name=name,
    )(a, w, res, g.reshape(1, n))
    return outs[0] if final else outs


def _attn_kernel(rb_ref, bkt_ref, q_ref, k_ref, v_ref, side_in_ref, o_ref, side_out_ref,
                 kmean_ref, kaug_ref, vt_ref, bias_ref, qaug_ref, s0_ref, s1_ref, p1_ref, acc_ref,
                 *, nb, qb, n_sel):
    bs = MOBA_BLOCK
    hd = HEAD_DIM
    qw = qb * bs
    h = pl.program_id(0)
    i0 = pl.program_id(1) * qb
    neg_inf = jnp.float32(-jnp.inf)
    side_out_ref[...] = side_in_ref[...].astype(BF16)

    @pl.when(pl.program_id(1) == 0)
    def _per_head_setup():
        bkt = bkt_ref[...]
        bias_row = jnp.zeros(bkt.shape, F32)
        for b in range(NUM_BUCKETS):
            bias_row = jnp.where(bkt == b, rb_ref[b, h] * LOG2E, bias_row)
        for d in range(nb):
            seg = jnp.broadcast_to(bias_row[:, d * bs:(d + 2) * bs], (bs, 2 * bs))
            tile = pltpu.roll(seg, 0, 1, stride=1, stride_axis=0)[:, bs:]
            if d == 0:
                kpos = lax.broadcasted_iota(jnp.int32, tile.shape, 0)
                qpos = lax.broadcasted_iota(jnp.int32, tile.shape, 1)
                tile = jnp.where(kpos <= qpos, tile, MASKED)
            bias_ref[d] = tile
        lane_blk = lax.broadcasted_iota(jnp.int32, (bs, V7X_LANES), 1)
        for j in range(nb):
            kj = k_ref[j * bs:(j + 1) * bs, :]
            kmean_ref[j:j + 1, :] = jnp.mean(kj.astype(F32), axis=0, keepdims=True)
            kaug_ref[j * bs:(j + 1) * bs, :] = jnp.concatenate(
                [kj, jnp.where(lane_blk == j, 1.0, 0.0).astype(BF16)], axis=1)
            ones_row = jnp.where(lax.broadcasted_iota(jnp.int32, (ATTN_SUM_ROWS, bs), 0) == 0, 1.0, 0.0)
            vt_ref[j] = jnp.concatenate([v_ref[j * bs:(j + 1) * bs, :].astype(F32).T, ones_row], axis=0).astype(BF16)

    qt = q_ref[...].astype(F32).T.astype(BF16)
    qaug_ref[:hd, :] = qt

    gate = jnp.dot(kmean_ref[...].astype(BF16), qt, preferred_element_type=F32)
    blk = lax.broadcasted_iota(jnp.int32, gate.shape, 0)
    cur = i0 + lax.broadcasted_iota(jnp.int32, gate.shape, 1) // bs
    gate = jnp.where(blk < cur, gate, neg_inf)
    madd = jnp.where(blk == cur, 0.0, MASKED).astype(F32)
    for _ in range(n_sel):
        top = jnp.max(gate, axis=0, keepdims=True)
        idx = jnp.min(jnp.where(gate == top, blk, nb), axis=0, keepdims=True)
        hit = blk == idx
        madd = jnp.where(jnp.logical_and(hit, top > neg_inf), 0.0, madd)
        gate = jnp.where(hit, neg_inf, gate)
    qaug_ref[hd:, :] = jnp.concatenate([madd, jnp.zeros((V7X_LANES - nb, qw), F32)], axis=0).astype(BF16)

    def scores(j, lo):
        kj = kaug_ref[pl.ds(pl.multiple_of(j * bs, bs), bs), :]
        s = jnp.dot(kj, qaug_ref[:, lo:], preferred_element_type=F32)
        bias = jnp.concatenate([bias_ref[jnp.maximum(i0 + a - j, 0)] for a in range(lo // bs, qb)], axis=1)
        return s + bias


    def own_block(u, m, s_cur_ref, s_next_ref, p_prev):
        lo, lo_prev = u * bs, max(u - 1, 0) * bs
        s = s_cur_ref[:, lo:]
        m_new = jnp.maximum(m[:, lo:], jnp.max(s, axis=0, keepdims=True))
        p = jnp.exp2(s - m_new).astype(BF16)
        if p_prev is not None:
            pv_prev = jnp.dot(vt_ref[i0 + u - 1], p_prev, preferred_element_type=F32)
            acc_ref[:, lo_prev:lo] = acc_ref[:, lo_prev:lo] + pv_prev[:, :lo - lo_prev]
            acc_ref[:, lo:] = jnp.exp2(m[:, lo:] - m_new) * (acc_ref[:, lo:] + pv_prev[:, lo - lo_prev:])
        if u + 1 < qb:
            s_next_ref[:, lo + bs:] = scores(i0 + u + 1, lo + bs)
        else:
            s_next_ref[...] = scores(0, 0)
        return (m_new if lo == 0 else jnp.concatenate([m[:, :lo], m_new], axis=1)), p

    def past_block(j, m, s_cur_ref, s_next_ref, p_prev):
        pv_prev = jnp.dot(vt_ref[jnp.maximum(j - 1, 0)], p_prev, preferred_element_type=F32)
        s = s_cur_ref[...]
        m_new = jnp.maximum(m, jnp.max(s, axis=0, keepdims=True))
        p = jnp.exp2(s - m_new).astype(BF16)
        acc_ref[...] = jnp.exp2(m - m_new) * (acc_ref[...] + pv_prev)
        s_next_ref[...] = scores(jnp.minimum(j + 1, i0 - 1), 0)
        return m_new, p

    def past_group(g, m):
        p = p1_ref[...]
        for u in range(0, qb, 2):
            m, p = past_block(qb * g + u, m, s0_ref, s1_ref, p)
            m, p = past_block(qb * g + u + 1, m, s1_ref, s0_ref, p)
        p1_ref[...] = p
        return m

    s0_ref[...] = scores(i0, 0)
    p1_ref[...] = jnp.zeros(p1_ref.shape, BF16)
    acc_ref[...] = jnp.zeros(acc_ref.shape, F32)
    m = jnp.full((1, qw), MASKED, F32)
    p = None
    for u in range(0, qb, 2):
        m, p = own_block(u, m, s0_ref, s1_ref, p)
        m, p = own_block(u + 1, m, s1_ref, s0_ref, p)
    lo = (qb - 1) * bs
    acc_ref[:, lo:] = acc_ref[:, lo:] + jnp.dot(vt_ref[i0 + qb - 1], p, preferred_element_type=F32)
    lax.fori_loop(0, pl.program_id(1), past_group, m)
    acc = acc_ref[...] + jnp.dot(vt_ref[jnp.maximum(i0 - 1, 0)], p1_ref[...], preferred_element_type=F32)
    o_ref[...] = (acc[:hd] / acc[hd:hd + 1]).T.astype(o_ref.dtype)


def _t5_bucket(dist):
    n = jnp.maximum(dist, 0)
    max_exact = NUM_BUCKETS // 2
    nf = jnp.maximum(n, max_exact).astype(F32)
    large = max_exact + (jnp.log(nf / max_exact) / math.log(MAX_DISTANCE / max_exact)
                         * (NUM_BUCKETS - max_exact)).astype(jnp.int32)
    large = jnp.minimum(large, NUM_BUCKETS - 1)
    return jnp.where(n < max_exact, n, large)


def _moba_attention(qkv, rel_bias, side_w, side_layer):
    s, three_d = qkv.shape
    d = three_d // 3
    nh = d // HEAD_DIM
    bs = MOBA_BLOCK
    assert s % bs == 0 and d % HEAD_DIM == 0
    nb = s // bs
    n_sel = min(MOBA_TOPK, nb - 1)
    qb = _tile(nb, ATTN_QUERY_BLOCKS, 2)
    assert nb <= V7X_LANES, "the block one-hot code occupies one lane tile"
    bkt = _t5_bucket(jnp.arange(-bs, s, dtype=jnp.int32)).reshape(1, s + bs)
    kern = functools.partial(_attn_kernel, nb=nb, qb=qb, n_sel=n_sel)
    n_ib = nb // qb
    side_in, side_out, side_shape = _side_cast_specs(side_w, side_layer, nh * n_ib, lambda h, i: h * n_ib + i)
    return pl.pallas_call(
        kern,
        grid=(nh, n_ib),
        in_specs=[
            pl.BlockSpec(memory_space=pltpu.SMEM),
            pl.BlockSpec((1, s + bs), lambda h, i: (0, 0)),
            pl.BlockSpec((qb * bs, HEAD_DIM), lambda h, i: (i, h)),
            pl.BlockSpec((s, HEAD_DIM), lambda h, i: (0, nh + h)),
            pl.BlockSpec((s, HEAD_DIM), lambda h, i: (0, 2 * nh + h)),
            side_in,
        ],
        out_specs=(pl.BlockSpec((qb * bs, HEAD_DIM), lambda h, i: (i, h)), side_out),
        out_shape=(jax.ShapeDtypeStruct((s, d), BF16), side_shape),
        scratch_shapes=[
            pltpu.VMEM((nb, HEAD_DIM), F32),
            pltpu.VMEM((s, HEAD_DIM + V7X_LANES), BF16),
            pltpu.VMEM((nb, HEAD_DIM + ATTN_SUM_ROWS, bs), BF16),
            pltpu.VMEM((nb, bs, bs), F32),
            pltpu.VMEM((HEAD_DIM + V7X_LANES, qb * bs), BF16),
            pltpu.VMEM((bs, qb * bs), F32),
            pltpu.VMEM((bs, qb * bs), F32),
            pltpu.VMEM((bs, qb * bs), BF16),
            pltpu.VMEM((HEAD_DIM + ATTN_SUM_ROWS, qb * bs), F32),
        ],
        compiler_params=_params("arbitrary", "arbitrary"),
        name="moba_attention",
    )(rel_bias, bkt, qkv, qkv, qkv, side_w)


def _sigmoid(x):
    return 0.5 * jnp.tanh(0.5 * x) + 0.5


def _rglru_kernel(xz_ref, cw_ref, cb_ref, wrg_ref, brg_ref, wig_ref, big_ref, lam_ref, side_in_ref,
                  y_ref, side_out_ref, xbuf_ref, xc_ref, a_ref, b_ref, h_ref, *, tt, c):
    pad = V7X_SUBLANES
    side_out_ref[...] = side_in_ref[...].astype(BF16)

    @pl.when(pl.program_id(0) == 0)
    def _():
        xbuf_ref[0:pad, :] = jnp.zeros((pad, c), F32)
        h_ref[...] = jnp.zeros((1, c), F32)

    xbuf_ref[pad:pad + tt, :] = xz_ref[:, :c]
    xc = cb_ref[...]
    for j in range(CONV_WIDTH):
        lag = CONV_WIDTH - 1 - j
        xc = xc + xbuf_ref[pad - lag:pad - lag + tt, :] * cw_ref[j:j + 1, :]
    xc_ref[...] = xc
    xbuf_ref[0:pad, :] = xbuf_ref[tt:tt + pad, :]

    for n in range(c // GATE_BLOCK):
        sl = slice(n * GATE_BLOCK, (n + 1) * GATE_BLOCK)
        xcn = xc_ref[:, sl]
        xb = xcn.astype(BF16)
        r = _sigmoid(jnp.dot(xb, wrg_ref[n], preferred_element_type=F32) + brg_ref[:, sl])
        ig = _sigmoid(jnp.dot(xb, wig_ref[n], preferred_element_type=F32) + big_ref[:, sl])
        log_a = (LRU_C * r) * jax.nn.log_sigmoid(lam_ref[:, sl])
        a = jnp.exp(log_a)
        a_ref[:, sl] = a
        w = -jnp.tanh(log_a) * (1.0 + a * a)
        b_ref[:, sl] = jnp.where(w > 0.0, w * lax.rsqrt(w), 0.0) * (ig * xcn)

    def step(t, hprev):
        hnew = a_ref[pl.ds(t, 1), :] * hprev + b_ref[pl.ds(t, 1), :]
        b_ref[pl.ds(t, 1), :] = hnew
        return hnew

    h_ref[...] = lax.fori_loop(0, tt, step, h_ref[...], unroll=8)
    y_ref[...] = (b_ref[...] * jax.nn.gelu(xz_ref[:, c:])).astype(y_ref.dtype)


def _rglru(xz, conv_w, conv_b, w_rg, b_rg, w_ig, b_ig, lam, side_w, side_layer):
    s, two_c = xz.shape
    c = two_c // 2
    assert c % GATE_BLOCK == 0
    ngb = c // GATE_BLOCK
    tt = _tile(s, 256, V7X_SUBLANES)
    kern = functools.partial(_rglru_kernel, tt=tt, c=c)
    vec = pl.BlockSpec((1, c), lambda t: (0, 0))
    gate_w = pl.BlockSpec((ngb, GATE_BLOCK, GATE_BLOCK), lambda t: (0, 0, 0))
    side_in, side_out, side_shape = _side_cast_specs(side_w, side_layer, s // tt, lambda t: t)
    return pl.pallas_call(
        kern,
        grid=(s // tt,),
        in_specs=[
            pl.BlockSpec((tt, two_c), lambda t: (t, 0)),
            pl.BlockSpec((CONV_WIDTH, c), lambda t: (0, 0)),
            vec, gate_w, vec, gate_w, vec, vec, side_in,
        ],
        out_specs=(pl.BlockSpec((tt, c), lambda t: (t, 0)), side_out),
        out_shape=(jax.ShapeDtypeStruct((s, c), BF16), side_shape),
        scratch_shapes=[
            pltpu.VMEM((tt + 2 * V7X_SUBLANES, c), F32),
            pltpu.VMEM((tt, c), F32),
            pltpu.VMEM((tt, c), F32),
            pltpu.VMEM((tt, c), F32),
            pltpu.VMEM((1, c), F32),
        ],
        compiler_params=_params("arbitrary"),
        name="rglru_core",
    )(xz, conv_w, conv_b.reshape(1, c), w_rg.astype(BF16), b_rg.reshape(1, c),
      w_ig.astype(BF16), b_ig.reshape(1, c), lam.reshape(1, c), side_w)


def kernel(x, rel_bias, mix_norm, mlp_norm, final_norm, attn_w_qkv, attn_w_o, rec_w_in, rec_conv_w,
           rec_conv_b, rec_w_rg, rec_b_rg, rec_w_ig, rec_b_ig, rec_lambda, rec_w_out, mlp_w1, mlp_w2):
    b, s, d = x.shape
    assert b == 1, "the trunk kernels handle one sequence"
    depth = mix_norm.shape[0]
    xr = x.reshape(s, d)
    xn = _rmsnorm(xr, mix_norm[0])
    out = None
    for layer in range(depth):
        j = layer // 2
        if layer % 2 == 0:
            qkv = _proj(xn, attn_w_qkv, j, BF16, "attn_qkv_proj",
                        epilogue="scale_leading", scaled_cols=d, scale=HEAD_DIM ** -0.5 * LOG2E)
            mixed, w2 = _moba_attention(qkv, rel_bias, mlp_w2, layer)
            w_out = attn_w_o
        else:
            xz = _proj(xn, rec_w_in, j, F32, "rec_in_proj")
            mixed, w2 = _rglru(xz, rec_conv_w[j], rec_conv_b[j], rec_w_rg[j], rec_b_rg[j],
                               rec_w_ig[j], rec_b_ig[j], rec_lambda[j], mlp_w2, layer)
            w_out = rec_w_out
        xr, xn = _resid_proj(mixed, w_out, j, xr, mlp_norm[layer], "mixer_out_proj")
        hidden = _proj(xn, mlp_w1, layer, BF16, "mlp_up_proj", epilogue="relu2")
        if layer + 1 < depth:
            xr, xn = _resid_proj(hidden, w2[None], 0, xr, mix_norm[layer + 1], "mlp_down_proj")
        else:
            out = _resid_proj(hidden, w2[None], 0, xr, final_norm, "mlp_down_proj_final", final=True)
    return out.reshape(b, s, d)
```

```python
import functools
import math

import jax
import jax.numpy as jnp
from jax import lax
from jax.experimental import pallas as pl
from jax.experimental.pallas import tpu as pltpu

HEAD_DIM = 128
MOBA_BLOCK = 256
MOBA_TOPK = 3
NUM_BUCKETS = 32
MAX_DISTANCE = 4096
CONV_WIDTH = 4
LRU_C = 8.0
GATE_BLOCK = 128
EPS = 1e-6
MASKED = -1e30
LOG2E = math.log2(math.e)
ATTN_QUERY_BLOCKS = 4
ATTN_SUM_ROWS = 16

V7X_VMEM_BYTES = 64 * 1024 * 1024
V7X_LANES = 128
V7X_SUBLANES = 8
VMEM_LIMIT_BYTES = V7X_VMEM_BYTES - 8 * 1024 * 1024

F32 = jnp.float32
BF16 = jnp.bfloat16


def _tile(dim, preferred, align):
    if dim <= preferred:
        return dim
    t = (preferred // align) * align
    while t > align and dim % t:
        t -= align
    assert dim % t == 0, (dim, preferred, align)
    return t


def _params(*semantics):
    return pltpu.CompilerParams(dimension_semantics=semantics, vmem_limit_bytes=VMEM_LIMIT_BYTES)


def _rms_rows(x, g):
    y = x * lax.rsqrt(jnp.mean(x * x, axis=-1, keepdims=True) + EPS)
    return y * g


def _norm_kernel(x_ref, g_ref, o_ref):
    o_ref[...] = _rms_rows(x_ref[...], g_ref[...]).astype(o_ref.dtype)


def _rmsnorm(x, g):
    m, d = x.shape
    tm = _tile(m, 512, V7X_SUBLANES)
    return pl.pallas_call(
        _norm_kernel,
        grid=(m // tm,),
        in_specs=[pl.BlockSpec((tm, d), lambda i: (i, 0)), pl.BlockSpec((1, d), lambda i: (0, 0))],
        out_specs=pl.BlockSpec((tm, d), lambda i: (i, 0)),
        out_shape=jax.ShapeDtypeStruct((m, d), BF16),
        compiler_params=_params("arbitrary"),
        name="rmsnorm_in",
    )(x, g.reshape(1, d))


def _side_cast_specs(w, layer, n_steps, step_index):
    _, k, n = w.shape
    slab = k // n_steps
    assert slab * n_steps == k and slab % (2 * V7X_SUBLANES) == 0, (k, n_steps)
    return (pl.BlockSpec((None, slab, n), lambda *g: (layer, step_index(*g), 0)),
            pl.BlockSpec((slab, n), lambda *g: (step_index(*g), 0)),
            jax.ShapeDtypeStruct((k, n), BF16))


def _proj_kernel(a_ref, w_ref, o_ref, wb_ref, *, epilogue, scaled_tiles, scale):
    @pl.when(pl.program_id(1) == 0)
    def _():
        wb_ref[...] = w_ref[...].astype(BF16)

    acc = jnp.dot(a_ref[...], wb_ref[...], preferred_element_type=F32)
    if epilogue == "relu2":
        acc = jnp.square(jnp.maximum(acc, 0.0))
    elif epilogue == "scale_leading":
        acc = acc * jnp.where(pl.program_id(0) < scaled_tiles, scale, 1.0).astype(F32)
    o_ref[...] = acc.astype(o_ref.dtype)


def _proj(a, w, layer, out_dtype, name, epilogue="none", scaled_cols=0, scale=1.0):
    m, k = a.shape
    n = w.shape[2]
    tn = _tile(math.gcd(n, scaled_cols) if scaled_cols else n, 1024, V7X_LANES)
    fixed_bytes = k * tn * (2 * 4 + 2)
    row_bytes = 2 * k * 2 + 2 * tn * jnp.dtype(out_dtype).itemsize + 4 * tn
    tm = _tile(m, min(2048, (VMEM_LIMIT_BYTES - fixed_bytes) // row_bytes // V7X_LANES * V7X_LANES), V7X_SUBLANES)
    kern = functools.partial(_proj_kernel, epilogue=epilogue, scaled_tiles=scaled_cols // tn, scale=scale)
    return pl.pallas_call(
        kern,
        grid=(n // tn, m // tm),
        in_specs=[pl.BlockSpec((tm, k), lambda j, i: (i, 0)),
                  pl.BlockSpec((None, k, tn), lambda j, i: (layer, 0, j))],
        out_specs=pl.BlockSpec((tm, tn), lambda j, i: (i, j)),
        out_shape=jax.ShapeDtypeStruct((m, n), out_dtype),
        scratch_shapes=[pltpu.VMEM((k, tn), BF16)],
        compiler_params=_params("arbitrary", "arbitrary"),
        name=name,
    )(a, w)


def _resid_kernel(a_ref, w_ref, res_ref, g_ref, *refs, final, cast_weight):
    n_out = 1 if final else 2
    out_refs = refs[:n_out]
    if cast_weight:
        wb_ref = refs[n_out]

        @pl.when(pl.program_id(0) == 0)
        def _():
            wb_ref[...] = w_ref[...].astype(BF16)
    else:
        wb_ref = w_ref

    x = res_ref[...] + jnp.dot(a_ref[...], wb_ref[...], preferred_element_type=F32)
    y = _rms_rows(x, g_ref[...])
    if final:
        out_refs[0][...] = y
    else:
        out_refs[0][...] = x
        out_refs[1][...] = y.astype(out_refs[1].dtype)


def _resid_proj(a, w, layer, res, g, name, final=False):
    m, k = a.shape
    n = w.shape[2]
    cast_weight = w.dtype != BF16
    weight_bytes = k * n * (w.dtype.itemsize + (2 if cast_weight else 0))
    row_bytes = 2 * (2 * k + 4 * n + (4 * n if final else 6 * n)) + 4 * n
    tm = _tile(m, max(V7X_LANES, (VMEM_LIMIT_BYTES - weight_bytes) // row_bytes // V7X_LANES * V7X_LANES),
               V7X_SUBLANES)
    kern = functools.partial(_resid_kernel, final=final, cast_weight=cast_weight)
    row_spec = pl.BlockSpec((tm, n), lambda i: (i, 0))
    if final:
        out_shape = (jax.ShapeDtypeStruct((m, n), F32),)
        out_specs = (row_spec,)
    else:
        out_shape = (jax.ShapeDtypeStruct((m, n), F32), jax.ShapeDtypeStruct((m, n), BF16))
        out_specs = (row_spec, row_spec)
    outs = pl.pallas_call(
        kern,
        grid=(m // tm,),
        in_specs=[
            pl.BlockSpec((tm, k), lambda i: (i, 0)),
            pl.BlockSpec((None, k, n), lambda i: (layer, 0, 0), pipeline_mode=pl.Buffered(1)),
            row_spec,
            pl.BlockSpec((1, n), lambda i: (0, 0)),
        ],
        out_specs=out_specs,
        out_shape=out_shape,
        scratch_shapes=[pltpu.VMEM((k, n), BF16)] if cast_weight else [],
        compiler_params=_params("arbitrary"),
        name=name,
    )(a, w, res, g.reshape(1, n))
    return outs[0] if final else outs


def _attn_kernel(rb_ref, bkt_ref, q_ref, k_ref, v_ref, side_in_ref, o_ref, side_out_ref,
                 kmean_ref, kaug_ref, vt_ref, bias_ref, qaug_ref, s0_ref, s1_ref, p1_ref, acc_ref,
                 *, nb, qb, n_sel):
    bs = MOBA_BLOCK
    hd = HEAD_DIM
    qw = qb * bs
    h = pl.program_id(0)
    i0 = pl.program_id(1) * qb
    neg_inf = jnp.float32(-jnp.inf)
    side_out_ref[...] = side_in_ref[...].astype(BF16)

    @pl.when(pl.program_id(1) == 0)
    def _per_head_setup():
        bkt = bkt_ref[...]
        bias_row = jnp.zeros(bkt.shape, F32)
        for b in range(NUM_BUCKETS):
            bias_row = jnp.where(bkt == b, rb_ref[b, h] * LOG2E, bias_row)
        for d in range(nb):
            seg = jnp.broadcast_to(bias_row[:, d * bs:(d + 2) * bs], (bs, 2 * bs))
            tile = pltpu.roll(seg, 0, 1, stride=1, stride_axis=0)[:, bs:]
            if d == 0:
                kpos = lax.broadcasted_iota(jnp.int32, tile.shape, 0)
                qpos = lax.broadcasted_iota(jnp.int32, tile.shape, 1)
                tile = jnp.where(kpos <= qpos, tile, MASKED)
            bias_ref[d] = tile
        lane_blk = lax.broadcasted_iota(jnp.int32, (bs, V7X_LANES), 1)
        for j in range(nb):
            kj = k_ref[j * bs:(j + 1) * bs, :]
            kmean_ref[j:j + 1, :] = jnp.mean(kj.astype(F32), axis=0, keepdims=True)
            kaug_ref[j * bs:(j + 1) * bs, :] = jnp.concatenate(
                [kj, jnp.where(lane_blk == j, 1.0, 0.0).astype(BF16)], axis=1)
            ones_row = jnp.where(lax.broadcasted_iota(jnp.int32, (ATTN_SUM_ROWS, bs), 0) == 0, 1.0, 0.0)
            vt_ref[j] = jnp.concatenate([v_ref[j * bs:(j + 1) * bs, :].astype(F32).T, ones_row], axis=0).astype(BF16)

    qt = q_ref[...].astype(F32).T.astype(BF16)
    qaug_ref[:hd, :] = qt

    gate = jnp.dot(kmean_ref[...].astype(BF16), qt, preferred_element_type=F32)
    blk = lax.broadcasted_iota(jnp.int32, gate.shape, 0)
    cur = i0 + lax.broadcasted_iota(jnp.int32, gate.shape, 1) // bs
    gate = jnp.where(blk < cur, gate, neg_inf)
    madd = jnp.where(blk == cur, 0.0, MASKED).astype(F32)
    for _ in range(n_sel):
        top = jnp.max(gate, axis=0, keepdims=True)
        idx = jnp.min(jnp.where(gate == top, blk, nb), axis=0, keepdims=True)
        hit = blk == idx
        madd = jnp.where(jnp.logical_and(hit, top > neg_inf), 0.0, madd)
        gate = jnp.where(hit, neg_inf, gate)
    qaug_ref[hd:, :] = jnp.concatenate([madd, jnp.zeros((V7X_LANES - nb, qw), F32)], axis=0).astype(BF16)

    def scores(j, lo):
        kj = kaug_ref[pl.ds(pl.multiple_of(j * bs, bs), bs), :]
        s = jnp.dot(kj, qaug_ref[:, lo:], preferred_element_type=F32)
        bias = jnp.concatenate([bias_ref[jnp.maximum(i0 + a - j, 0)] for a in range(lo // bs, qb)], axis=1)
        return s + bias


    def own_block(u, m, s_cur_ref, s_next_ref, p_prev):
        lo, lo_prev = u * bs, max(u - 1, 0) * bs
        s = s_cur_ref[:, lo:]
        m_new = jnp.maximum(m[:, lo:], jnp.max(s, axis=0, keepdims=True))
        p = jnp.exp2(s - m_new).astype(BF16)
        if p_prev is not None:
            pv_prev = jnp.dot(vt_ref[i0 + u - 1], p_prev, preferred_element_type=F32)
            acc_ref[:, lo_prev:lo] = acc_ref[:, lo_prev:lo] + pv_prev[:, :lo - lo_prev]
            acc_ref[:, lo:] = jnp.exp2(m[:, lo:] - m_new) * (acc_ref[:, lo:] + pv_prev[:, lo - lo_prev:])
        if u + 1 < qb:
            s_next_ref[:, lo + bs:] = scores(i0 + u + 1, lo + bs)
        else:
            s_next_ref[...] = scores(0, 0)
        return (m_new if lo == 0 else jnp.concatenate([m[:, :lo], m_new], axis=1)), p

    def past_block(j, m, s_cur_ref, s_next_ref, p_prev):
        pv_prev = jnp.dot(vt_ref[jnp.maximum(j - 1, 0)], p_prev, preferred_element_type=F32)
        s = s_cur_ref[...]
        m_new = jnp.maximum(m, jnp.max(s, axis=0, keepdims=True))
        p = jnp.exp2(s - m_new).astype(BF16)
        acc_ref[...] = jnp.exp2(m - m_new) * (acc_ref[...] + pv_prev)
        s_next_ref[...] = scores(jnp.minimum(j + 1, i0 - 1), 0)
        return m_new, p

    def past_group(g, m):
        p = p1_ref[...]
        for u in range(0, qb, 2):
            m, p = past_block(qb * g + u, m, s0_ref, s1_ref, p)
            m, p = past_block(qb * g + u + 1, m, s1_ref, s0_ref, p)
        p1_ref[...] = p
        return m

    s0_ref[...] = scores(i0, 0)
    p1_ref[...] = jnp.zeros(p1_ref.shape, BF16)
    acc_ref[...] = jnp.zeros(acc_ref.shape, F32)
    m = jnp.full((1, qw), MASKED, F32)
    p = None
    for u in range(0, qb, 2):
        m, p = own_block(u, m, s0_ref, s1_ref, p)
        m, p = own_block(u + 1, m, s1_ref, s0_ref, p)
    lo = (qb - 1) * bs
    acc_ref[:, lo:] = acc_ref[:, lo:] + jnp.dot(vt_ref[i0 + qb - 1], p, preferred_element_type=F32)
    n_groups = pl.program_id(1)
    m = lax.fori_loop(0, n_groups // 2, lambda t, m: past_group(2 * t + 1, past_group(2 * t, m)), m)
    lax.fori_loop(n_groups // 2 * 2, n_groups, past_group, m)
    acc = acc_ref[...] + jnp.dot(vt_ref[jnp.maximum(i0 - 1, 0)], p1_ref[...], preferred_element_type=F32)
    o_ref[...] = (acc[:hd] / acc[hd:hd + 1]).T.astype(o_ref.dtype)


def _t5_bucket(dist):
    n = jnp.maximum(dist, 0)
    max_exact = NUM_BUCKETS // 2
    nf = jnp.maximum(n, max_exact).astype(F32)
    large = max_exact + (jnp.log(nf / max_exact) / math.log(MAX_DISTANCE / max_exact)
                         * (NUM_BUCKETS - max_exact)).astype(jnp.int32)
    large = jnp.minimum(large, NUM_BUCKETS - 1)
    return jnp.where(n < max_exact, n, large)


def _moba_attention(qkv, rel_bias, side_w, side_layer):
    s, three_d = qkv.shape
    d = three_d // 3
    nh = d // HEAD_DIM
    bs = MOBA_BLOCK
    assert s % bs == 0 and d % HEAD_DIM == 0
    nb = s // bs
    n_sel = min(MOBA_TOPK, nb - 1)
    qb = _tile(nb, ATTN_QUERY_BLOCKS, 2)
    assert nb <= V7X_LANES, "the block one-hot code occupies one lane tile"
    bkt = _t5_bucket(jnp.arange(-bs, s, dtype=jnp.int32)).reshape(1, s + bs)
    kern = functools.partial(_attn_kernel, nb=nb, qb=qb, n_sel=n_sel)
    n_ib = nb // qb
    side_in, side_out, side_shape = _side_cast_specs(side_w, side_layer, nh * n_ib, lambda h, i: h * n_ib + i)
    return pl.pallas_call(
        kern,
        grid=(nh, n_ib),
        in_specs=[
            pl.BlockSpec(memory_space=pltpu.SMEM),
            pl.BlockSpec((1, s + bs), lambda h, i: (0, 0)),
            pl.BlockSpec((qb * bs, HEAD_DIM), lambda h, i: (i, h)),
            pl.BlockSpec((s, HEAD_DIM), lambda h, i: (0, nh + h)),
            pl.BlockSpec((s, HEAD_DIM), lambda h, i: (0, 2 * nh + h)),
            side_in,
        ],
        out_specs=(pl.BlockSpec((qb * bs, HEAD_DIM), lambda h, i: (i, h)), side_out),
        out_shape=(jax.ShapeDtypeStruct((s, d), BF16), side_shape),
        scratch_shapes=[
            pltpu.VMEM((nb, HEAD_DIM), F32),
            pltpu.VMEM((s, HEAD_DIM + V7X_LANES), BF16),
            pltpu.VMEM((nb, HEAD_DIM + ATTN_SUM_ROWS, bs), BF16),
            pltpu.VMEM((nb, bs, bs), F32),
            pltpu.VMEM((HEAD_DIM + V7X_LANES, qb * bs), BF16),
            pltpu.VMEM((bs, qb * bs), F32),
            pltpu.VMEM((bs, qb * bs), F32),
            pltpu.VMEM((bs, qb * bs), BF16),
            pltpu.VMEM((HEAD_DIM + ATTN_SUM_ROWS, qb * bs), F32),
        ],
        compiler_params=_params("arbitrary", "arbitrary"),
        name="moba_attention",
    )(rel_bias, bkt, qkv, qkv, qkv, side_w)


def _sigmoid(x):
    return 0.5 * jnp.tanh(0.5 * x) + 0.5


def _rglru_kernel(xz_ref, cw_ref, cb_ref, wrg_ref, brg_ref, wig_ref, big_ref, lam_ref, side_in_ref,
                  y_ref, side_out_ref, xbuf_ref, xc_ref, a_ref, b_ref, h_ref, *, tt, c):
    pad = V7X_SUBLANES
    side_out_ref[...] = side_in_ref[...].astype(BF16)

    @pl.when(pl.program_id(0) == 0)
    def _():
        xbuf_ref[0:pad, :] = jnp.zeros((pad, c), F32)
        h_ref[...] = jnp.zeros((1, c), F32)

    xbuf_ref[pad:pad + tt, :] = xz_ref[:, :c]
    xc = cb_ref[...]
    for j in range(CONV_WIDTH):
        lag = CONV_WIDTH - 1 - j
        xc = xc + xbuf_ref[pad - lag:pad - lag + tt, :] * cw_ref[j:j + 1, :]
    xc_ref[...] = xc
    xbuf_ref[0:pad, :] = xbuf_ref[tt:tt + pad, :]

    for n in range(c // GATE_BLOCK):
        sl = slice(n * GATE_BLOCK, (n + 1) * GATE_BLOCK)
        xcn = xc_ref[:, sl]
        xb = xcn.astype(BF16)
        r = _sigmoid(jnp.dot(xb, wrg_ref[n], preferred_element_type=F32) + brg_ref[:, sl])
        ig = _sigmoid(jnp.dot(xb, wig_ref[n], preferred_element_type=F32) + big_ref[:, sl])
        log_a = (LRU_C * r) * jax.nn.log_sigmoid(lam_ref[:, sl])
        a = jnp.exp(log_a)
        a_ref[:, sl] = a
        w = -jnp.tanh(log_a) * (1.0 + a * a)
        b_ref[:, sl] = jnp.where(w > 0.0, w * lax.rsqrt(w), 0.0) * (ig * xcn)

    def step(t, hprev):
        hnew = a_ref[pl.ds(t, 1), :] * hprev + b_ref[pl.ds(t, 1), :]
        b_ref[pl.ds(t, 1), :] = hnew
        return hnew

    h_ref[...] = lax.fori_loop(0, tt, step, h_ref[...], unroll=8)
    y_ref[...] = (b_ref[...] * jax.nn.gelu(xz_ref[:, c:])).astype(y_ref.dtype)


def _rglru(xz, conv_w, conv_b, w_rg, b_rg, w_ig, b_ig, lam, side_w, side_layer):
    s, two_c = xz.shape
    c = two_c // 2
    assert c % GATE_BLOCK == 0
    ngb = c // GATE_BLOCK
    tt = _tile(s, 256, V7X_SUBLANES)
    kern = functools.partial(_rglru_kernel, tt=tt, c=c)
    vec = pl.BlockSpec((1, c), lambda t: (0, 0))
    gate_w = pl.BlockSpec((ngb, GATE_BLOCK, GATE_BLOCK), lambda t: (0, 0, 0))
    side_in, side_out, side_shape = _side_cast_specs(side_w, side_layer, s // tt, lambda t: t)
    return pl.pallas_call(
        kern,
        grid=(s // tt,),
        in_specs=[
            pl.BlockSpec((tt, two_c), lambda t: (t, 0)),
            pl.BlockSpec((CONV_WIDTH, c), lambda t: (0, 0)),
            vec, gate_w, vec, gate_w, vec, vec, side_in,
        ],
        out_specs=(pl.BlockSpec((tt, c), lambda t: (t, 0)), side_out),
        out_shape=(jax.ShapeDtypeStruct((s, c), BF16), side_shape),
        scratch_shapes=[
            pltpu.VMEM((tt + 2 * V7X_SUBLANES, c), F32),
            pltpu.VMEM((tt, c), F32),
            pltpu.VMEM((tt, c), F32),
            pltpu.VMEM((tt, c), F32),
            pltpu.VMEM((1, c), F32),
        ],
        compiler_params=_params("arbitrary"),
        name="rglru_core",
    )(xz, conv_w, conv_b.reshape(1, c), w_rg.astype(BF16), b_rg.reshape(1, c),
      w_ig.astype(BF16), b_ig.reshape(1, c), lam.reshape(1, c), side_w)


def kernel(x, rel_bias, mix_norm, mlp_norm, final_norm, attn_w_qkv, attn_w_o, rec_w_in, rec_conv_w,
           rec_conv_b, rec_w_rg, rec_b_rg, rec_w_ig, rec_b_ig, rec_lambda, rec_w_out, mlp_w1, mlp_w2):
    b, s, d = x.shape
    assert b == 1, "the trunk kernels handle one sequence"
    depth = mix_norm.shape[0]
    xr = x.reshape(s, d)
    xn = _rmsnorm(xr, mix_norm[0])
    out = None
    for layer in range(depth):
        j = layer // 2
        if layer % 2 == 0:
            qkv = _proj(xn, attn_w_qkv, j, BF16, "attn_qkv_proj",
                        epilogue="scale_leading", scaled_cols=d, scale=HEAD_DIM ** -0.5 * LOG2E)
            mixed, w2 = _moba_attention(qkv, rel_bias, mlp_w2, layer)
            w_out = attn_w_o
        else:
            xz = _proj(xn, rec_w_in, j, F32, "rec_in_proj")
            mixed, w2 = _rglru(xz, rec_conv_w[j], rec_conv_b[j], rec_w_rg[j], rec_b_rg[j],
                               rec_w_ig[j], rec_b_ig[j], rec_lambda[j], mlp_w2, layer)
            w_out = rec_w_out
        xr, xn = _resid_proj(mixed, w_out, j, xr, mlp_norm[layer], "mixer_out_proj")
        hidden = _proj(xn, mlp_w1, layer, BF16, "mlp_up_proj", epilogue="relu2")
        if layer + 1 < depth:
            xr, xn = _resid_proj(hidden, w2[None], 0, xr, mix_norm[layer + 1], "mlp_down_proj")
        else:
            out = _resid_proj(hidden, w2[None], 0, xr, final_norm, "mlp_down_proj_final", final=True)
    return out.reshape(b, s, d)
```

```python
import functools
import math

import jax
import jax.numpy as jnp
from jax import lax
from jax.experimental import pallas as pl
from jax.experimental.pallas import tpu as pltpu

HEAD_DIM = 128
MOBA_BLOCK = 256
MOBA_TOPK = 3
NUM_BUCKETS = 32
MAX_DISTANCE = 4096
CONV_WIDTH = 4
LRU_C = 8.0
GATE_BLOCK = 128
EPS = 1e-6
MASKED = -1e30
LOG2E = math.log2(math.e)
ATTN_QUERY_BLOCKS = 8
ATTN_SUM_ROWS = 16

V7X_VMEM_BYTES = 64 * 1024 * 1024
V7X_LANES = 128
V7X_SUBLANES = 8
VMEM_LIMIT_BYTES = V7X_VMEM_BYTES - 8 * 1024 * 1024

F32 = jnp.float32
BF16 = jnp.bfloat16


def _tile(dim, preferred, align):
    if dim <= preferred:
        return dim
    t = (preferred // align) * align
    while t > align and dim % t:
        t -= align
    assert dim % t == 0, (dim, preferred, align)
    return t


def _params(*semantics):
    return pltpu.CompilerParams(dimension_semantics=semantics, vmem_limit_bytes=VMEM_LIMIT_BYTES)


def _rms_rows(x, g):
    y = x * lax.rsqrt(jnp.mean(x * x, axis=-1, keepdims=True) + EPS)
    return y * g


def _norm_kernel(x_ref, g_ref, o_ref):
    o_ref[...] = _rms_rows(x_ref[...], g_ref[...]).astype(o_ref.dtype)


def _rmsnorm(x, g):
    m, d = x.shape
    tm = _tile(m, 512, V7X_SUBLANES)
    return pl.pallas_call(
        _norm_kernel,
        grid=(m // tm,),
        in_specs=[pl.BlockSpec((tm, d), lambda i: (i, 0)), pl.BlockSpec((1, d), lambda i: (0, 0))],
        out_specs=pl.BlockSpec((tm, d), lambda i: (i, 0)),
        out_shape=jax.ShapeDtypeStruct((m, d), BF16),
        compiler_params=_params("arbitrary"),
        name="rmsnorm_in",
    )(x, g.reshape(1, d))


def _side_cast_specs(w, layer, n_steps, step_index):
    _, k, n = w.shape
    slab = k // n_steps
    assert slab * n_steps == k and slab % (2 * V7X_SUBLANES) == 0, (k, n_steps)
    return (pl.BlockSpec((None, slab, n), lambda *g: (layer, step_index(*g), 0)),
            pl.BlockSpec((slab, n), lambda *g: (step_index(*g), 0)),
            jax.ShapeDtypeStruct((k, n), BF16))


def _proj_kernel(a_ref, w_ref, o_ref, wb_ref, *, epilogue, scaled_tiles, scale):
    @pl.when(pl.program_id(1) == 0)
    def _():
        wb_ref[...] = w_ref[...].astype(BF16)

    acc = jnp.dot(a_ref[...], wb_ref[...], preferred_element_type=F32)
    if epilogue == "relu2":
        acc = jnp.square(jnp.maximum(acc, 0.0))
    elif epilogue == "scale_leading":
        acc = acc * jnp.where(pl.program_id(0) < scaled_tiles, scale, 1.0).astype(F32)
    o_ref[...] = acc.astype(o_ref.dtype)


def _proj(a, w, layer, out_dtype, name, epilogue="none", scaled_cols=0, scale=1.0):
    m, k = a.shape
    n = w.shape[2]
    tn = _tile(math.gcd(n, scaled_cols) if scaled_cols else n, 1024, V7X_LANES)
    fixed_bytes = k * tn * (2 * 4 + 2)
    row_bytes = 2 * k * 2 + 2 * tn * jnp.dtype(out_dtype).itemsize + 4 * tn
    tm = _tile(m, min(2048, (VMEM_LIMIT_BYTES - fixed_bytes) // row_bytes // V7X_LANES * V7X_LANES), V7X_SUBLANES)
    kern = functools.partial(_proj_kernel, epilogue=epilogue, scaled_tiles=scaled_cols // tn, scale=scale)
    return pl.pallas_call(
        kern,
        grid=(n // tn, m // tm),
        in_specs=[pl.BlockSpec((tm, k), lambda j, i: (i, 0)),
                  pl.BlockSpec((None, k, tn), lambda j, i: (layer, 0, j))],
        out_specs=pl.BlockSpec((tm, tn), lambda j, i: (i, j)),
        out_shape=jax.ShapeDtypeStruct((m, n), out_dtype),
        scratch_shapes=[pltpu.VMEM((k, tn), BF16)],
        compiler_params=_params("arbitrary", "arbitrary"),
        name=name,
    )(a, w)


def _resid_kernel(a_ref, w_ref, res_ref, g_ref, *refs, final, cast_weight):
    n_out = 1 if final else 2
    out_refs = refs[:n_out]
    if cast_weight:
        wb_ref = refs[n_out]

        @pl.when(pl.program_id(0) == 0)
        def _():
            wb_ref[...] = w_ref[...].astype(BF16)
    else:
        wb_ref = w_ref

    x = res_ref[...] + jnp.dot(a_ref[...], wb_ref[...], preferred_element_type=F32)
    y = _rms_rows(x, g_ref[...])
    if final:
        out_refs[0][...] = y
    else:
        out_refs[0][...] = x
        out_refs[1][...] = y.astype(out_refs[1].dtype)


def _resid_proj(a, w, layer, res, g, name, final=False):
    m, k = a.shape
    n = w.shape[2]
    cast_weight = w.dtype != BF16
    weight_bytes = k * n * (w.dtype.itemsize + (2 if cast_weight else 0))
    row_bytes = 2 * (2 * k + 4 * n + (4 * n if final else 6 * n)) + 4 * n
    tm = _tile(m, max(V7X_LANES, (VMEM_LIMIT_BYTES - weight_bytes) // row_bytes // V7X_LANES * V7X_LANES),
               V7X_SUBLANES)
    kern = functools.partial(_resid_kernel, final=final, cast_weight=cast_weight)
    row_spec = pl.BlockSpec((tm, n), lambda i: (i, 0))
    if final:
        out_shape = (jax.ShapeDtypeStruct((m, n), F32),)
        out_specs = (row_spec,)
    else:
        out_shape = (jax.ShapeDtypeStruct((m, n), F32), jax.ShapeDtypeStruct((m, n), BF16))
        out_specs = (row_spec, row_spec)
    outs = pl.pallas_call(
        kern,
        grid=(m // tm,),
        in_specs=[
            pl.BlockSpec((tm, k), lambda i: (i, 0)),
            pl.BlockSpec((None, k, n), lambda i: (layer, 0, 0), pipeline_mode=pl.Buffered(1)),
            row_spec,
            pl.BlockSpec((1, n), lambda i: (0, 0)),
        ],
        out_specs=out_specs,
        out_shape=out_shape,
        scratch_shapes=[pltpu.VMEM((k, n), BF16)] if cast_weight else [],
        compiler_params=_params("arbitrary"),
        name=name,
    )(a, w, res, g.reshape(1, n))
    return outs[0] if final else outs


def _attn_kernel(rb_ref, bkt_ref, q_ref, k_ref, v_ref, side_in_ref, o_ref, side_out_ref,
                 kmean_ref, kaug_ref, vt_ref, bias_ref, qaug_ref, s0_ref, s1_ref, p1_ref, acc_ref,
                 *, nb, qb, n_sel):
    bs = MOBA_BLOCK
    hd = HEAD_DIM
    qw = qb * bs
    h = pl.program_id(0)
    i0 = pl.program_id(1) * qb
    neg_inf = jnp.float32(-jnp.inf)
    side_out_ref[...] = side_in_ref[...].astype(BF16)

    @pl.when(pl.program_id(1) == 0)
    def _per_head_setup():
        bkt = bkt_ref[...]
        bias_row = jnp.zeros(bkt.shape, F32)
        for b in range(NUM_BUCKETS):
            bias_row = jnp.where(bkt == b, rb_ref[b, h] * LOG2E, bias_row)
        for d in range(nb):
            seg = jnp.broadcast_to(bias_row[:, d * bs:(d + 2) * bs], (bs, 2 * bs))
            tile = pltpu.roll(seg, 0, 1, stride=1, stride_axis=0)[:, bs:]
            if d == 0:
                kpos = lax.broadcasted_iota(jnp.int32, tile.shape, 0)
                qpos = lax.broadcasted_iota(jnp.int32, tile.shape, 1)
                tile = jnp.where(kpos <= qpos, tile, MASKED)
            bias_ref[d] = tile
        lane_blk = lax.broadcasted_iota(jnp.int32, (bs, V7X_LANES), 1)
        for j in range(nb):
            kj = k_ref[j * bs:(j + 1) * bs, :]
            kmean_ref[j:j + 1, :] = jnp.mean(kj.astype(F32), axis=0, keepdims=True)
            kaug_ref[j * bs:(j + 1) * bs, :] = jnp.concatenate(
                [kj, jnp.where(lane_blk == j, 1.0, 0.0).astype(BF16)], axis=1)
            ones_row = jnp.where(lax.broadcasted_iota(jnp.int32, (ATTN_SUM_ROWS, bs), 0) == 0, 1.0, 0.0)
            vt_ref[j] = jnp.concatenate([v_ref[j * bs:(j + 1) * bs, :].astype(F32).T, ones_row], axis=0).astype(BF16)

    qt = q_ref[...].astype(F32).T.astype(BF16)
    qaug_ref[:hd, :] = qt

    gate = jnp.dot(kmean_ref[...].astype(BF16), qt, preferred_element_type=F32)
    blk = lax.broadcasted_iota(jnp.int32, gate.shape, 0)
    cur = i0 + lax.broadcasted_iota(jnp.int32, gate.shape, 1) // bs
    gate = jnp.where(blk < cur, gate, neg_inf)
    madd = jnp.where(blk == cur, 0.0, MASKED).astype(F32)
    for _ in range(n_sel):
        top = jnp.max(gate, axis=0, keepdims=True)
        idx = jnp.min(jnp.where(gate == top, blk, nb), axis=0, keepdims=True)
        hit = blk == idx
        madd = jnp.where(jnp.logical_and(hit, top > neg_inf), 0.0, madd)
        gate = jnp.where(hit, neg_inf, gate)
    qaug_ref[hd:, :] = jnp.concatenate([madd, jnp.zeros((V7X_LANES - nb, qw), F32)], axis=0).astype(BF16)

    def scores(j, lo):
        kj = kaug_ref[pl.ds(pl.multiple_of(j * bs, bs), bs), :]
        s = jnp.dot(kj, qaug_ref[:, lo:], preferred_element_type=F32)
        bias = jnp.concatenate([bias_ref[jnp.maximum(i0 + a - j, 0)] for a in range(lo // bs, qb)], axis=1)
        return s + bias


    def own_block(u, m, s_cur_ref, s_next_ref, p_prev):
        lo, lo_prev = u * bs, max(u - 1, 0) * bs
        s = s_cur_ref[:, lo:]
        m_new = jnp.maximum(m[:, lo:], jnp.max(s, axis=0, keepdims=True))
        p = jnp.exp2(s - m_new).astype(BF16)
        if p_prev is not None:
            pv_prev = jnp.dot(vt_ref[i0 + u - 1], p_prev, preferred_element_type=F32)
            acc_ref[:, lo_prev:lo] = acc_ref[:, lo_prev:lo] + pv_prev[:, :lo - lo_prev]
            acc_ref[:, lo:] = jnp.exp2(m[:, lo:] - m_new) * (acc_ref[:, lo:] + pv_prev[:, lo - lo_prev:])
        if u + 1 < qb:
            s_next_ref[:, lo + bs:] = scores(i0 + u + 1, lo + bs)
        else:
            s_next_ref[...] = scores(0, 0)
        return (m_new if lo == 0 else jnp.concatenate([m[:, :lo], m_new], axis=1)), p

    def past_block(j, m, s_cur_ref, s_next_ref, p_prev):
        pv_prev = jnp.dot(vt_ref[jnp.maximum(j - 1, 0)], p_prev, preferred_element_type=F32)
        s = s_cur_ref[...]
        m_new = jnp.maximum(m, jnp.max(s, axis=0, keepdims=True))
        p = jnp.exp2(s - m_new).astype(BF16)
        acc_ref[...] = jnp.exp2(m - m_new) * (acc_ref[...] + pv_prev)
        s_next_ref[...] = scores(jnp.minimum(j + 1, i0 - 1), 0)
        return m_new, p

    def past_group(g, m):
        p = p1_ref[...]
        for u in range(0, qb, 2):
            m, p = past_block(qb * g + u, m, s0_ref, s1_ref, p)
            m, p = past_block(qb * g + u + 1, m, s1_ref, s0_ref, p)
        p1_ref[...] = p
        return m

    s0_ref[...] = scores(i0, 0)
    p1_ref[...] = jnp.zeros(p1_ref.shape, BF16)
    acc_ref[...] = jnp.zeros(acc_ref.shape, F32)
    m = jnp.full((1, qw), MASKED, F32)
    p = None
    for u in range(0, qb, 2):
        m, p = own_block(u, m, s0_ref, s1_ref, p)
        m, p = own_block(u + 1, m, s1_ref, s0_ref, p)
    lo = (qb - 1) * bs
    acc_ref[:, lo:] = acc_ref[:, lo:] + jnp.dot(vt_ref[i0 + qb - 1], p, preferred_element_type=F32)
    lax.fori_loop(0, pl.program_id(1), past_group, m)
    acc = acc_ref[...] + jnp.dot(vt_ref[jnp.maximum(i0 - 1, 0)], p1_ref[...], preferred_element_type=F32)
    o_ref[...] = (acc[:hd] / acc[hd:hd + 1]).T.astype(o_ref.dtype)


def _t5_bucket(dist):
    n = jnp.maximum(dist, 0)
    max_exact = NUM_BUCKETS // 2
    nf = jnp.maximum(n, max_exact).astype(F32)
    large = max_exact + (jnp.log(nf / max_exact) / math.log(MAX_DISTANCE / max_exact)
                         * (NUM_BUCKETS - max_exact)).astype(jnp.int32)
    large = jnp.minimum(large, NUM_BUCKETS - 1)
    return jnp.where(n < max_exact, n, large)


def _moba_attention(qkv, rel_bias, side_w, side_layer):
    s, three_d = qkv.shape
    d = three_d // 3
    nh = d // HEAD_DIM
    bs = MOBA_BLOCK
    assert s % bs == 0 and d % HEAD_DIM == 0
    nb = s // bs
    n_sel = min(MOBA_TOPK, nb - 1)
    qb = _tile(nb, ATTN_QUERY_BLOCKS, 2)
    assert nb <= V7X_LANES, "the block one-hot code occupies one lane tile"
    bkt = _t5_bucket(jnp.arange(-bs, s, dtype=jnp.int32)).reshape(1, s + bs)
    kern = functools.partial(_attn_kernel, nb=nb, qb=qb, n_sel=n_sel)
    n_ib = nb // qb
    side_in, side_out, side_shape = _side_cast_specs(side_w, side_layer, nh * n_ib, lambda h, i: h * n_ib + i)
    return pl.pallas_call(
        kern,
        grid=(nh, n_ib),
        in_specs=[
            pl.BlockSpec(memory_space=pltpu.SMEM),
            pl.BlockSpec((1, s + bs), lambda h, i: (0, 0)),
            pl.BlockSpec((qb * bs, HEAD_DIM), lambda h, i: (i, h)),
            pl.BlockSpec((s, HEAD_DIM), lambda h, i: (0, nh + h)),
            pl.BlockSpec((s, HEAD_DIM), lambda h, i: (0, 2 * nh + h)),
            side_in,
        ],
        out_specs=(pl.BlockSpec((qb * bs, HEAD_DIM), lambda h, i: (i, h)), side_out),
        out_shape=(jax.ShapeDtypeStruct((s, d), BF16), side_shape),
        scratch_shapes=[
            pltpu.VMEM((nb, HEAD_DIM), F32),
            pltpu.VMEM((s, HEAD_DIM + V7X_LANES), BF16),
            pltpu.VMEM((nb, HEAD_DIM + ATTN_SUM_ROWS, bs), BF16),
            pltpu.VMEM((nb, bs, bs), F32),
            pltpu.VMEM((HEAD_DIM + V7X_LANES, qb * bs), BF16),
            pltpu.VMEM((bs, qb * bs), F32),
            pltpu.VMEM((bs, qb * bs), F32),
            pltpu.VMEM((bs, qb * bs), BF16),
            pltpu.VMEM((HEAD_DIM + ATTN_SUM_ROWS, qb * bs), F32),
        ],
        compiler_params=_params("arbitrary", "arbitrary"),
        name="moba_attention",
    )(rel_bias, bkt, qkv, qkv, qkv, side_w)


def _sigmoid(x):
    return 0.5 * jnp.tanh(0.5 * x) + 0.5


def _rglru_kernel(xz_ref, cw_ref, cb_ref, wrg_ref, brg_ref, wig_ref, big_ref, lam_ref, side_in_ref,
                  y_ref, side_out_ref, xbuf_ref, xc_ref, a_ref, b_ref, h_ref, *, tt, c):
    pad = V7X_SUBLANES
    side_out_ref[...] = side_in_ref[...].astype(BF16)

    @pl.when(pl.program_id(0) == 0)
    def _():
        xbuf_ref[0:pad, :] = jnp.zeros((pad, c), F32)
        h_ref[...] = jnp.zeros((1, c), F32)

    xbuf_ref[pad:pad + tt, :] = xz_ref[:, :c]
    xc = cb_ref[...]
    for j in range(CONV_WIDTH):
        lag = CONV_WIDTH - 1 - j
        xc = xc + xbuf_ref[pad - lag:pad - lag + tt, :] * cw_ref[j:j + 1, :]
    xc_ref[...] = xc
    xbuf_ref[0:pad, :] = xbuf_ref[tt:tt + pad, :]

    for n in range(c // GATE_BLOCK):
        sl = slice(n * GATE_BLOCK, (n + 1) * GATE_BLOCK)
        xcn = xc_ref[:, sl]
        xb = xcn.astype(BF16)
        r = _sigmoid(jnp.dot(xb, wrg_ref[n], preferred_element_type=F32) + brg_ref[:, sl])
        ig = _sigmoid(jnp.dot(xb, wig_ref[n], preferred_element_type=F32) + big_ref[:, sl])
        log_a = (LRU_C * r) * jax.nn.log_sigmoid(lam_ref[:, sl])
        a = jnp.exp(log_a)
        a_ref[:, sl] = a
        w = -jnp.tanh(log_a) * (1.0 + a * a)
        b_ref[:, sl] = jnp.where(w > 0.0, w * lax.rsqrt(w), 0.0) * (ig * xcn)

    def step(t, hprev):
        hnew = a_ref[pl.ds(t, 1), :] * hprev + b_ref[pl.ds(t, 1), :]
        b_ref[pl.ds(t, 1), :] = hnew
        return hnew

    h_ref[...] = lax.fori_loop(0, tt, step, h_ref[...], unroll=8)
    y_ref[...] = (b_ref[...] * jax.nn.gelu(xz_ref[:, c:])).astype(y_ref.dtype)


def _rglru(xz, conv_w, conv_b, w_rg, b_rg, w_ig, b_ig, lam, side_w, side_layer):
    s, two_c = xz.shape
    c = two_c // 2
    assert c % GATE_BLOCK == 0
    ngb = c // GATE_BLOCK
    tt = _tile(s, 256, V7X_SUBLANES)
    kern = functools.partial(_rglru_kernel, tt=tt, c=c)
    vec = pl.BlockSpec((1, c), lambda t: (0, 0))
    gate_w = pl.BlockSpec((ngb, GATE_BLOCK, GATE_BLOCK), lambda t: (0, 0, 0))
    side_in, side_out, side_shape = _side_cast_specs(side_w, side_layer, s // tt, lambda t: t)
    return pl.pallas_call(
        kern,
        grid=(s // tt,),
        in_specs=[
            pl.BlockSpec((tt, two_c), lambda t: (t, 0)),
            pl.BlockSpec((CONV_WIDTH, c), lambda t: (0, 0)),
            vec, gate_w, vec, gate_w, vec, vec, side_in,
        ],
        out_specs=(pl.BlockSpec((tt, c), lambda t: (t, 0)), side_out),
        out_shape=(jax.ShapeDtypeStruct((s, c), BF16), side_shape),
        scratch_shapes=[
            pltpu.VMEM((tt + 2 * V7X_SUBLANES, c), F32),
            pltpu.VMEM((tt, c), F32),
            pltpu.VMEM((tt, c), F32),
            pltpu.VMEM((tt, c), F32),
            pltpu.VMEM((1, c), F32),
        ],
        compiler_params=_params("arbitrary"),
        name="rglru_core",
    )(xz, conv_w, conv_b.reshape(1, c), w_rg.astype(BF16), b_rg.reshape(1, c),
      w_ig.astype(BF16), b_ig.reshape(1, c), lam.reshape(1, c), side_w)


def kernel(x, rel_bias, mix_norm, mlp_norm, final_norm, attn_w_qkv, attn_w_o, rec_w_in, rec_conv_w,
           rec_conv_b, rec_w_rg, rec_b_rg, rec_w_ig, rec_b_ig, rec_lambda, rec_w_out, mlp_w1, mlp_w2):
    b, s, d = x.shape
    assert b == 1, "the trunk kernels handle one sequence"
    depth = mix_norm.shape[0]
    xr = x.reshape(s, d)
    xn = _rmsnorm(xr, mix_norm[0])
    out = None
    for layer in range(depth):
        j = layer // 2
        if layer % 2 == 0:
            qkv = _proj(xn, attn_w_qkv, j, BF16, "attn_qkv_proj",
                        epilogue="scale_leading", scaled_cols=d, scale=HEAD_DIM ** -0.5 * LOG2E)
            mixed, w2 = _moba_attention(qkv, rel_bias, mlp_w2, layer)
            w_out = attn_w_o
        else:
            xz = _proj(xn, rec_w_in, j, F32, "rec_in_proj")
            mixed, w2 = _rglru(xz, rec_conv_w[j], rec_conv_b[j], rec_w_rg[j], rec_b_rg[j],
                               rec_w_ig[j], rec_b_ig[j], rec_lambda[j], mlp_w2, layer)
            w_out = rec_w_out
        xr, xn = _resid_proj(mixed, w_out, j, xr, mlp_norm[layer], "mixer_out_proj")
        hidden = _proj(xn, mlp_w1, layer, BF16, "mlp_up_proj", epilogue="relu2")
        if layer + 1 < depth:
            xr, xn = _resid_proj(hidden, w2[None], 0, xr, mix_norm[layer + 1], "mlp_down_proj")
        else:
            out = _resid_proj(hidden, w2[None], 0, xr, final_norm, "mlp_down_proj_final", final=True)
    return out.reshape(b, s, d)
```

```python
import functools
import math

import jax
import jax.numpy as jnp
from jax import lax
from jax.experimental import pallas as pl
from jax.experimental.pallas import tpu as pltpu

HEAD_DIM = 128
MOBA_BLOCK = 256
MOBA_TOPK = 3
NUM_BUCKETS = 32
MAX_DISTANCE = 4096
CONV_WIDTH = 4
LRU_C = 8.0
GATE_BLOCK = 128
EPS = 1e-6
GELU_C1 = math.sqrt(2.0 / math.pi)
GELU_C3 = 0.044715
MASKED = -1e30
LOG2E = math.log2(math.e)
ATTN_QUERY_BLOCKS = 8
ATTN_SUM_ROWS = 16

V7X_VMEM_BYTES = 64 * 1024 * 1024
V7X_LANES = 128
V7X_SUBLANES = 8
VMEM_LIMIT_BYTES = V7X_VMEM_BYTES - 8 * 1024 * 1024

F32 = jnp.float32
BF16 = jnp.bfloat16


def _tile(dim, preferred, align):
    if dim <= preferred:
        return dim
    t = (preferred // align) * align
    while t > align and dim % t:
        t -= align
    assert dim % t == 0, (dim, preferred, align)
    return t


def _params(*semantics):
    return pltpu.CompilerParams(dimension_semantics=semantics, vmem_limit_bytes=VMEM_LIMIT_BYTES)


def _rms_rows(x, g):
    y = x * lax.rsqrt(jnp.mean(x * x, axis=-1, keepdims=True) + EPS)
    return y * g


def _norm_kernel(x_ref, g_ref, o_ref):
    o_ref[...] = _rms_rows(x_ref[...], g_ref[...]).astype(o_ref.dtype)


def _rmsnorm(x, g):
    m, d = x.shape
    tm = _tile(m, 512, V7X_SUBLANES)
    return pl.pallas_call(
        _norm_kernel,
        grid=(m // tm,),
        in_specs=[pl.BlockSpec((tm, d), lambda i: (i, 0)), pl.BlockSpec((1, d), lambda i: (0, 0))],
        out_specs=pl.BlockSpec((tm, d), lambda i: (i, 0)),
        out_shape=jax.ShapeDtypeStruct((m, d), BF16),
        compiler_params=_params("arbitrary"),
        name="rmsnorm_in",
    )(x, g.reshape(1, d))


def _side_cast_specs(w, layer, n_steps, step_index):
    _, k, n = w.shape
    slab = k // n_steps
    assert slab * n_steps == k and slab % (2 * V7X_SUBLANES) == 0, (k, n_steps)
    return (pl.BlockSpec((None, slab, n), lambda *g: (layer, step_index(*g), 0)),
            pl.BlockSpec((slab, n), lambda *g: (step_index(*g), 0)),
            jax.ShapeDtypeStruct((k, n), BF16))


def _proj_kernel(a_ref, w_ref, o_ref, wb_ref, *, epilogue, scaled_tiles, scale):
    @pl.when(pl.program_id(1) == 0)
    def _():
        wb_ref[...] = w_ref[...].astype(BF16)

    acc = jnp.dot(a_ref[...], wb_ref[...], preferred_element_type=F32)
    if epilogue == "relu2":
        acc = jnp.square(jnp.maximum(acc, 0.0))
    elif epilogue == "scale_leading":
        acc = acc * jnp.where(pl.program_id(0) < scaled_tiles, scale, 1.0).astype(F32)
    o_ref[...] = acc.astype(o_ref.dtype)


def _proj(a, w, layer, out_dtype, name, epilogue="none", scaled_cols=0, scale=1.0):
    m, k = a.shape
    n = w.shape[2]
    tn = _tile(math.gcd(n, scaled_cols) if scaled_cols else n, 1024, V7X_LANES)
    fixed_bytes = k * tn * (2 * 4 + 2)
    row_bytes = 2 * k * 2 + 2 * tn * jnp.dtype(out_dtype).itemsize + 4 * tn
    tm = _tile(m, min(2048, (VMEM_LIMIT_BYTES - fixed_bytes) // row_bytes // V7X_LANES * V7X_LANES), V7X_SUBLANES)
    kern = functools.partial(_proj_kernel, epilogue=epilogue, scaled_tiles=scaled_cols // tn, scale=scale)
    return pl.pallas_call(
        kern,
        grid=(n // tn, m // tm),
        in_specs=[pl.BlockSpec((tm, k), lambda j, i: (i, 0)),
                  pl.BlockSpec((None, k, tn), lambda j, i: (layer, 0, j))],
        out_specs=pl.BlockSpec((tm, tn), lambda j, i: (i, j)),
        out_shape=jax.ShapeDtypeStruct((m, n), out_dtype),
        scratch_shapes=[pltpu.VMEM((k, tn), BF16)],
        compiler_params=_params("arbitrary", "arbitrary"),
        name=name,
    )(a, w)


def _resid_kernel(a_ref, w_ref, res_ref, g_ref, *refs, final, cast_weight):
    n_out = 1 if final else 2
    out_refs = refs[:n_out]
    if cast_weight:
        wb_ref = refs[n_out]

        @pl.when(pl.program_id(0) == 0)
        def _():
            wb_ref[...] = w_ref[...].astype(BF16)
    else:
        wb_ref = w_ref

    x = res_ref[...] + jnp.dot(a_ref[...], wb_ref[...], preferred_element_type=F32)
    y = _rms_rows(x, g_ref[...])
    if final:
        out_refs[0][...] = y
    else:
        out_refs[0][...] = x
        out_refs[1][...] = y.astype(out_refs[1].dtype)


def _resid_proj(a, w, layer, res, g, name, final=False):
    m, k = a.shape
    n = w.shape[2]
    cast_weight = w.dtype != BF16
    weight_bytes = k * n * (w.dtype.itemsize + (2 if cast_weight else 0))
    row_bytes = 2 * (2 * k + 4 * n + (4 * n if final else 6 * n)) + 4 * n
    tm = _tile(m, max(V7X_LANES, (VMEM_LIMIT_BYTES - weight_bytes) // row_bytes // V7X_LANES * V7X_LANES),
               V7X_SUBLANES)
    kern = functools.partial(_resid_kernel, final=final, cast_weight=cast_weight)
    row_spec = pl.BlockSpec((tm, n), lambda i: (i, 0))
    if final:
        out_shape = (jax.ShapeDtypeStruct((m, n), F32),)
        out_specs = (row_spec,)
    else:
        out_shape = (jax.ShapeDtypeStruct((m, n), F32), jax.ShapeDtypeStruct((m, n), BF16))
        out_specs = (row_spec, row_spec)
    outs = pl.pallas_call(
        kern,
        grid=(m // tm,),
        in_specs=[
            pl.BlockSpec((tm, k), lambda i: (i, 0)),
            pl.BlockSpec((None, k, n), lambda i: (layer, 0, 0), pipeline_mode=pl.Buffered(1)),
            row_spec,
            pl.BlockSpec((1, n), lambda i: (0, 0)),
        ],
        out_specs=out_specs,
        out_shape=out_shape,
        scratch_shapes=[pltpu.VMEM((k, n), BF16)] if cast_weight else [],
        compiler_params=_params("arbitrary"),
        name=name,
    )(a, w, res, g.reshape(1, n))
    return outs[0] if final else outs


def _attn_kernel(rb_ref, bkt_ref, q_ref, k_ref, v_ref, side_in_ref, o_ref, side_out_ref,
                 kmean_ref, kaug_ref, vt_ref, bias_ref, qaug_ref, s0_ref, s1_ref, p1_ref, acc_ref,
                 *, nb, qb, n_sel):
    bs = MOBA_BLOCK
    hd = HEAD_DIM
    qw = qb * bs
    h = pl.program_id(0)
    i0 = pl.program_id(1) * qb
    neg_inf = jnp.float32(-jnp.inf)
    side_out_ref[...] = side_in_ref[...].astype(BF16)

    @pl.when(pl.program_id(1) == 0)
    def _per_head_setup():
        bkt = bkt_ref[...]
        bias_row = jnp.zeros(bkt.shape, F32)
        for b in range(NUM_BUCKETS):
            bias_row = jnp.where(bkt == b, rb_ref[b, h] * LOG2E, bias_row)
        for d in range(nb):
            seg = jnp.broadcast_to(bias_row[:, d * bs:(d + 2) * bs], (bs, 2 * bs))
            tile = pltpu.roll(seg, 0, 1, stride=1, stride_axis=0)[:, bs:]
            if d == 0:
                kpos = lax.broadcasted_iota(jnp.int32, tile.shape, 0)
                qpos = lax.broadcasted_iota(jnp.int32, tile.shape, 1)
                tile = jnp.where(kpos <= qpos, tile, MASKED)
            bias_ref[d] = tile
        lane_blk = lax.broadcasted_iota(jnp.int32, (bs, V7X_LANES), 1)
        for j in range(nb):
            kj = k_ref[j * bs:(j + 1) * bs, :]
            kmean_ref[j:j + 1, :] = jnp.mean(kj.astype(F32), axis=0, keepdims=True)
            kaug_ref[j * bs:(j + 1) * bs, :] = jnp.concatenate(
                [kj, jnp.where(lane_blk == j, 1.0, 0.0).astype(BF16)], axis=1)
            ones_row = jnp.where(lax.broadcasted_iota(jnp.int32, (ATTN_SUM_ROWS, bs), 0) == 0, 1.0, 0.0)
            vt_ref[j] = jnp.concatenate([v_ref[j * bs:(j + 1) * bs, :].astype(F32).T, ones_row], axis=0).astype(BF16)

    qt = q_ref[...].astype(F32).T.astype(BF16)
    qaug_ref[:hd, :] = qt

    gate = jnp.dot(kmean_ref[...].astype(BF16), qt, preferred_element_type=F32)
    blk = lax.broadcasted_iota(jnp.int32, gate.shape, 0)
    cur = i0 + lax.broadcasted_iota(jnp.int32, gate.shape, 1) // bs
    gate = jnp.where(blk < cur, gate, neg_inf)
    madd = jnp.where(blk == cur, 0.0, MASKED).astype(F32)
    for _ in range(n_sel):
        top = jnp.max(gate, axis=0, keepdims=True)
        idx = jnp.min(jnp.where(gate == top, blk, nb), axis=0, keepdims=True)
        hit = blk == idx
        madd = jnp.where(jnp.logical_and(hit, top > neg_inf), 0.0, madd)
        gate = jnp.where(hit, neg_inf, gate)
    qaug_ref[hd:, :] = jnp.concatenate([madd, jnp.zeros((V7X_LANES - nb, qw), F32)], axis=0).astype(BF16)

    def scores(j, lo):
        kj = kaug_ref[pl.ds(pl.multiple_of(j * bs, bs), bs), :]
        s = jnp.dot(kj, qaug_ref[:, lo:], preferred_element_type=F32)
        bias = jnp.concatenate([bias_ref[jnp.maximum(i0 + a - j, 0)] for a in range(lo // bs, qb)], axis=1)
        return s + bias


    def own_block(u, m, s_cur_ref, s_next_ref, p_prev):
        lo, lo_prev = u * bs, max(u - 1, 0) * bs
        s = s_cur_ref[:, lo:]
        m_new = jnp.maximum(m[:, lo:], jnp.max(s, axis=0, keepdims=True))
        p = jnp.exp2(s - m_new).astype(BF16)
        if p_prev is not None:
            pv_prev = jnp.dot(vt_ref[i0 + u - 1], p_prev, preferred_element_type=F32)
            acc_ref[:, lo_prev:lo] = acc_ref[:, lo_prev:lo] + pv_prev[:, :lo - lo_prev]
            acc_ref[:, lo:] = jnp.exp2(m[:, lo:] - m_new) * (acc_ref[:, lo:] + pv_prev[:, lo - lo_prev:])
        if u + 1 < qb:
            s_next_ref[:, lo + bs:] = scores(i0 + u + 1, lo + bs)
        else:
            s_next_ref[...] = scores(0, 0)
        return (m_new if lo == 0 else jnp.concatenate([m[:, :lo], m_new], axis=1)), p

    def past_block(j, m, s_cur_ref, s_next_ref, p_prev):
        pv_prev = jnp.dot(vt_ref[jnp.maximum(j - 1, 0)], p_prev, preferred_element_type=F32)
        s = s_cur_ref[...]
        m_new = jnp.maximum(m, jnp.max(s, axis=0, keepdims=True))
        p = jnp.exp2(s - m_new).astype(BF16)
        acc_ref[...] = jnp.exp2(m - m_new) * (acc_ref[...] + pv_prev)
        s_next_ref[...] = scores(jnp.minimum(j + 1, i0 - 1), 0)
        return m_new, p

    def past_group(g, m):
        p = p1_ref[...]
        for u in range(0, qb, 2):
            m, p = past_block(qb * g + u, m, s0_ref, s1_ref, p)
            m, p = past_block(qb * g + u + 1, m, s1_ref, s0_ref, p)
        p1_ref[...] = p
        return m

    s0_ref[...] = scores(i0, 0)
    p1_ref[...] = jnp.zeros(p1_ref.shape, BF16)
    acc_ref[...] = jnp.zeros(acc_ref.shape, F32)
    m = jnp.full((1, qw), MASKED, F32)
    p = None
    for u in range(0, qb, 2):
        m, p = own_block(u, m, s0_ref, s1_ref, p)
        m, p = own_block(u + 1, m, s1_ref, s0_ref, p)
    lo = (qb - 1) * bs
    acc_ref[:, lo:] = acc_ref[:, lo:] + jnp.dot(vt_ref[i0 + qb - 1], p, preferred_element_type=F32)
    lax.fori_loop(0, pl.program_id(1), past_group, m)
    acc = acc_ref[...] + jnp.dot(vt_ref[jnp.maximum(i0 - 1, 0)], p1_ref[...], preferred_element_type=F32)
    o_ref[...] = (acc[:hd] / acc[hd:hd + 1]).T.astype(o_ref.dtype)


def _t5_bucket(dist):
    n = jnp.maximum(dist, 0)
    max_exact = NUM_BUCKETS // 2
    nf = jnp.maximum(n, max_exact).astype(F32)
    large = max_exact + (jnp.log(nf / max_exact) / math.log(MAX_DISTANCE / max_exact)
                         * (NUM_BUCKETS - max_exact)).astype(jnp.int32)
    large = jnp.minimum(large, NUM_BUCKETS - 1)
    return jnp.where(n < max_exact, n, large)


def _moba_attention(qkv, rel_bias, side_w, side_layer):
    s, three_d = qkv.shape
    d = three_d // 3
    nh = d // HEAD_DIM
    bs = MOBA_BLOCK
    assert s % bs == 0 and d % HEAD_DIM == 0
    nb = s // bs
    n_sel = min(MOBA_TOPK, nb - 1)
    qb = _tile(nb, ATTN_QUERY_BLOCKS, 2)
    assert nb <= V7X_LANES, "the block one-hot code occupies one lane tile"
    bkt = _t5_bucket(jnp.arange(-bs, s, dtype=jnp.int32)).reshape(1, s + bs)
    kern = functools.partial(_attn_kernel, nb=nb, qb=qb, n_sel=n_sel)
    n_ib = nb // qb
    side_in, side_out, side_shape = _side_cast_specs(side_w, side_layer, nh * n_ib, lambda h, i: h * n_ib + i)
    return pl.pallas_call(
        kern,
        grid=(nh, n_ib),
        in_specs=[
            pl.BlockSpec(memory_space=pltpu.SMEM),
            pl.BlockSpec((1, s + bs), lambda h, i: (0, 0)),
            pl.BlockSpec((qb * bs, HEAD_DIM), lambda h, i: (i, h)),
            pl.BlockSpec((s, HEAD_DIM), lambda h, i: (0, nh + h)),
            pl.BlockSpec((s, HEAD_DIM), lambda h, i: (0, 2 * nh + h)),
            side_in,
        ],
        out_specs=(pl.BlockSpec((qb * bs, HEAD_DIM), lambda h, i: (i, h)), side_out),
        out_shape=(jax.ShapeDtypeStruct((s, d), BF16), side_shape),
        scratch_shapes=[
            pltpu.VMEM((nb, HEAD_DIM), F32),
            pltpu.VMEM((s, HEAD_DIM + V7X_LANES), BF16),
            pltpu.VMEM((nb, HEAD_DIM + ATTN_SUM_ROWS, bs), BF16),
            pltpu.VMEM((nb, bs, bs), F32),
            pltpu.VMEM((HEAD_DIM + V7X_LANES, qb * bs), BF16),
            pltpu.VMEM((bs, qb * bs), F32),
            pltpu.VMEM((bs, qb * bs), F32),
            pltpu.VMEM((bs, qb * bs), BF16),
            pltpu.VMEM((HEAD_DIM + ATTN_SUM_ROWS, qb * bs), F32),
        ],
        compiler_params=_params("arbitrary", "arbitrary"),
        name="moba_attention",
    )(rel_bias, bkt, qkv, qkv, qkv, side_w)


def _sigmoid(x):
    return 0.5 * jnp.tanh(0.5 * x) + 0.5


def _rglru_kernel(xz_ref, cw_ref, cb_ref, wrg_ref, brg_ref, wig_ref, big_ref, lam_ref, side_in_ref,
                  y_ref, side_out_ref, xbuf_ref, xc_ref, a_ref, b_ref, h_ref, *, tt, c):
    pad = V7X_SUBLANES
    side_out_ref[...] = side_in_ref[...].astype(BF16)

    @pl.when(pl.program_id(0) == 0)
    def _():
        xbuf_ref[0:pad, :] = jnp.zeros((pad, c), F32)
        h_ref[...] = jnp.zeros((1, c), F32)

    xbuf_ref[pad:pad + tt, :] = xz_ref[:, :c]
    xc = cb_ref[...]
    for j in range(CONV_WIDTH):
        lag = CONV_WIDTH - 1 - j
        xc = xc + xbuf_ref[pad - lag:pad - lag + tt, :] * cw_ref[j:j + 1, :]
    xc_ref[...] = xc
    xbuf_ref[0:pad, :] = xbuf_ref[tt:tt + pad, :]

    for n in range(c // GATE_BLOCK):
        sl = slice(n * GATE_BLOCK, (n + 1) * GATE_BLOCK)
        xcn = xc_ref[:, sl]
        xb = xcn.astype(BF16)
        t_r = jnp.tanh(0.5 * (jnp.dot(xb, wrg_ref[n], preferred_element_type=F32) + brg_ref[:, sl]))
        ig = _sigmoid(jnp.dot(xb, wig_ref[n], preferred_element_type=F32) + big_ref[:, sl])
        half = (0.5 * LRU_C) * jax.nn.log_sigmoid(lam_ref[:, sl])
        log_a = t_r * half + half
        a = jnp.exp(log_a)
        a_ref[:, sl] = a
        w = -jnp.tanh(log_a) * (1.0 + a * a)
        b_ref[:, sl] = jnp.where(w > 0.0, w * lax.rsqrt(w), 0.0) * (ig * xcn)

    def step(t, hprev):
        hnew = a_ref[pl.ds(t, 1), :] * hprev + b_ref[pl.ds(t, 1), :]
        b_ref[pl.ds(t, 1), :] = hnew
        return hnew

    h_ref[...] = lax.fori_loop(0, tt, step, h_ref[...], unroll=8)
    g = xz_ref[:, c:]
    inner = g * (GELU_C1 + (GELU_C1 * GELU_C3) * (g * g))
    y_ref[...] = ((b_ref[...] * (0.5 * g)) * (1.0 + jnp.tanh(inner))).astype(y_ref.dtype)


def _rglru(xz, conv_w, conv_b, w_rg, b_rg, w_ig, b_ig, lam, side_w, side_layer):
    s, two_c = xz.shape
    c = two_c // 2
    assert c % GATE_BLOCK == 0
    ngb = c // GATE_BLOCK
    tt = _tile(s, 256, V7X_SUBLANES)
    kern = functools.partial(_rglru_kernel, tt=tt, c=c)
    vec = pl.BlockSpec((1, c), lambda t: (0, 0))
    gate_w = pl.BlockSpec((ngb, GATE_BLOCK, GATE_BLOCK), lambda t: (0, 0, 0))
    side_in, side_out, side_shape = _side_cast_specs(side_w, side_layer, s // tt, lambda t: t)
    return pl.pallas_call(
        kern,
        grid=(s // tt,),
        in_specs=[
            pl.BlockSpec((tt, two_c), lambda t: (t, 0)),
            pl.BlockSpec((CONV_WIDTH, c), lambda t: (0, 0)),
            vec, gate_w, vec, gate_w, vec, vec, side_in,
        ],
        out_specs=(pl.BlockSpec((tt, c), lambda t: (t, 0)), side_out),
        out_shape=(jax.ShapeDtypeStruct((s, c), BF16), side_shape),
        scratch_shapes=[
            pltpu.VMEM((tt + 2 * V7X_SUBLANES, c), F32),
            pltpu.VMEM((tt, c), F32),
            pltpu.VMEM((tt, c), F32),
            pltpu.VMEM((tt, c), F32),
            pltpu.VMEM((1, c), F32),
        ],
        compiler_params=_params("arbitrary"),
        name="rglru_core",
    )(xz, conv_w, conv_b.reshape(1, c), w_rg.astype(BF16), b_rg.reshape(1, c),
      w_ig.astype(BF16), b_ig.reshape(1, c), lam.reshape(1, c), side_w)


def kernel(x, rel_bias, mix_norm, mlp_norm, final_norm, attn_w_qkv, attn_w_o, rec_w_in, rec_conv_w,
           rec_conv_b, rec_w_rg, rec_b_rg, rec_w_ig, rec_b_ig, rec_lambda, rec_w_out, mlp_w1, mlp_w2):
    b, s, d = x.shape
    assert b == 1, "the trunk kernels handle one sequence"
    depth = mix_norm.shape[0]
    xr = x.reshape(s, d)
    xn = _rmsnorm(xr, mix_norm[0])
    out = None
    for layer in range(depth):
        j = layer // 2
        if layer % 2 == 0:
            qkv = _proj(xn, attn_w_qkv, j, BF16, "attn_qkv_proj",
                        epilogue="scale_leading", scaled_cols=d, scale=HEAD_DIM ** -0.5 * LOG2E)
            mixed, w2 = _moba_attention(qkv, rel_bias, mlp_w2, layer)
            w_out = attn_w_o
        else:
            xz = _proj(xn, rec_w_in, j, F32, "rec_in_proj")
            mixed, w2 = _rglru(xz, rec_conv_w[j], rec_conv_b[j], rec_w_rg[j], rec_b_rg[j],
                               rec_w_ig[j], rec_b_ig[j], rec_lambda[j], mlp_w2, layer)
            w_out = rec_w_out
        xr, xn = _resid_proj(mixed, w_out, j, xr, mlp_norm[layer], "mixer_out_proj")
        hidden = _proj(xn, mlp_w1, layer, BF16, "mlp_up_proj", epilogue="relu2")
        if layer + 1 < depth:
            xr, xn = _resid_proj(hidden, w2[None], 0, xr, mix_norm[layer + 1], "mlp_down_proj")
        else:
            out = _resid_proj(hidden, w2[None], 0, xr, final_norm, "mlp_down_proj_final", final=True)
    return out.reshape(b, s, d)
```

```python
import functools
import math

import jax
import jax.numpy as jnp
from jax import lax
from jax.experimental import pallas as pl
from jax.experimental.pallas import tpu as pltpu

HEAD_DIM = 128
MOBA_BLOCK = 256
MOBA_TOPK = 3
NUM_BUCKETS = 32
MAX_DISTANCE = 4096
CONV_WIDTH = 4
LRU_C = 8.0
GATE_BLOCK = 128
EPS = 1e-6
WEIGHT_DMA_CHUNKS = 4
GELU_C1 = math.sqrt(2.0 / math.pi)
GELU_C3 = 0.044715
MASKED = -1e30
LOG2E = math.log2(math.e)
ATTN_QUERY_BLOCKS = 8
ATTN_SUM_ROWS = 16

V7X_VMEM_BYTES = 64 * 1024 * 1024
V7X_LANES = 128
V7X_SUBLANES = 8
VMEM_LIMIT_BYTES = V7X_VMEM_BYTES - 8 * 1024 * 1024

F32 = jnp.float32
BF16 = jnp.bfloat16


def _tile(dim, preferred, align):
    if dim <= preferred:
        return dim
    t = (preferred // align) * align
    while t > align and dim % t:
        t -= align
    assert dim % t == 0, (dim, preferred, align)
    return t


def _params(*semantics):
    return pltpu.CompilerParams(dimension_semantics=semantics, vmem_limit_bytes=VMEM_LIMIT_BYTES)


def _rms_rows(x, g):
    y = x * lax.rsqrt(jnp.mean(x * x, axis=-1, keepdims=True) + EPS)
    return y * g


def _norm_kernel(x_ref, g_ref, o_ref):
    o_ref[...] = _rms_rows(x_ref[...], g_ref[...]).astype(o_ref.dtype)


def _rmsnorm(x, g):
    m, d = x.shape
    tm = _tile(m, 512, V7X_SUBLANES)
    return pl.pallas_call(
        _norm_kernel,
        grid=(m // tm,),
        in_specs=[pl.BlockSpec((tm, d), lambda i: (i, 0)), pl.BlockSpec((1, d), lambda i: (0, 0))],
        out_specs=pl.BlockSpec((tm, d), lambda i: (i, 0)),
        out_shape=jax.ShapeDtypeStruct((m, d), BF16),
        compiler_params=_params("arbitrary"),
        name="rmsnorm_in",
    )(x, g.reshape(1, d))


def _side_cast_specs(w, layer, n_steps, step_index):
    _, k, n = w.shape
    slab = k // n_steps
    assert slab * n_steps == k and slab % (2 * V7X_SUBLANES) == 0, (k, n_steps)
    return (pl.BlockSpec((None, slab, n), lambda *g: (layer, step_index(*g), 0)),
            pl.BlockSpec((slab, n), lambda *g: (step_index(*g), 0)),
            jax.ShapeDtypeStruct((k, n), BF16))


def _proj_kernel(a_ref, w_ref, o_ref, wb_ref, *, epilogue, scaled_tiles, scale):
    @pl.when(pl.program_id(1) == 0)
    def _():
        wb_ref[...] = w_ref[...].astype(BF16)

    acc = jnp.dot(a_ref[...], wb_ref[...], preferred_element_type=F32)
    if epilogue == "relu2":
        acc = jnp.square(jnp.maximum(acc, 0.0))
    elif epilogue == "scale_leading":
        acc = acc * jnp.where(pl.program_id(0) < scaled_tiles, scale, 1.0).astype(F32)
    o_ref[...] = acc.astype(o_ref.dtype)


def _proj(a, w, layer, out_dtype, name, epilogue="none", scaled_cols=0, scale=1.0):
    m, k = a.shape
    n = w.shape[2]
    tn = _tile(math.gcd(n, scaled_cols) if scaled_cols else n, 1024, V7X_LANES)
    fixed_bytes = k * tn * (2 * 4 + 2)
    row_bytes = 2 * k * 2 + 2 * tn * jnp.dtype(out_dtype).itemsize + 4 * tn
    tm = _tile(m, min(2048, (VMEM_LIMIT_BYTES - fixed_bytes) // row_bytes // V7X_LANES * V7X_LANES), V7X_SUBLANES)
    kern = functools.partial(_proj_kernel, epilogue=epilogue, scaled_tiles=scaled_cols // tn, scale=scale)
    return pl.pallas_call(
        kern,
        grid=(n // tn, m // tm),
        in_specs=[pl.BlockSpec((tm, k), lambda j, i: (i, 0)),
                  pl.BlockSpec((None, k, tn), lambda j, i: (layer, 0, j))],
        out_specs=pl.BlockSpec((tm, tn), lambda j, i: (i, j)),
        out_shape=jax.ShapeDtypeStruct((m, n), out_dtype),
        scratch_shapes=[pltpu.VMEM((k, tn), BF16)],
        compiler_params=_params("arbitrary", "arbitrary"),
        name=name,
    )(a, w)


def _resid_kernel(a_ref, w_ref, res_ref, g_ref, *refs, final, cast_weight, layer):
    n_out = 1 if final else 2
    out_refs = refs[:n_out]

    def finish(x):
        y = _rms_rows(x, g_ref[...])
        if final:
            out_refs[0][...] = y
        else:
            out_refs[0][...] = x
            out_refs[1][...] = y.astype(out_refs[1].dtype)

    if cast_weight:
        wb_ref = refs[n_out]

        @pl.when(pl.program_id(0) == 0)
        def _():
            wb_ref[...] = w_ref[...].astype(BF16)

        finish(res_ref[...] + jnp.dot(a_ref[...], wb_ref[...], preferred_element_type=F32))
        return

    wb_ref, sem = refs[n_out], refs[n_out + 1]
    ck = wb_ref.shape[0] // WEIGHT_DMA_CHUNKS

    def chunk_copy(c):
        rows = pl.ds(c * ck, ck)
        return pltpu.make_async_copy(w_ref.at[layer, rows, :], wb_ref.at[rows, :], sem.at[c])

    first = pl.program_id(0) == 0

    @pl.when(first)
    def _():
        for c in range(WEIGHT_DMA_CHUNKS):
            chunk_copy(c).start()
        x = res_ref[...]
        for c in range(WEIGHT_DMA_CHUNKS):
            chunk_copy(c).wait()
            x = x + jnp.dot(a_ref[:, c * ck:(c + 1) * ck], wb_ref[c * ck:(c + 1) * ck, :],
                            preferred_element_type=F32)
        finish(x)

    @pl.when(jnp.logical_not(first))
    def _():
        finish(res_ref[...] + jnp.dot(a_ref[...], wb_ref[...], preferred_element_type=F32))


def _resid_proj(a, w, layer, res, g, name, final=False):
    m, k = a.shape
    n = w.shape[2]
    cast_weight = w.dtype != BF16
    weight_bytes = k * n * (w.dtype.itemsize + (2 if cast_weight else 0))
    row_bytes = 2 * (2 * k + 4 * n + (4 * n if final else 6 * n)) + 4 * n
    tm = _tile(m, max(V7X_LANES, (VMEM_LIMIT_BYTES - weight_bytes) // row_bytes // V7X_LANES * V7X_LANES),
               V7X_SUBLANES)
    kern = functools.partial(_resid_kernel, final=final, cast_weight=cast_weight, layer=layer)
    if cast_weight:
        w_spec = pl.BlockSpec((None, k, n), lambda i: (layer, 0, 0), pipeline_mode=pl.Buffered(1))
        scratch = [pltpu.VMEM((k, n), BF16)]
    else:
        assert k % (WEIGHT_DMA_CHUNKS * V7X_LANES) == 0, k
        w_spec = pl.BlockSpec(memory_space=pl.ANY)
        scratch = [pltpu.VMEM((k, n), BF16), pltpu.SemaphoreType.DMA((WEIGHT_DMA_CHUNKS,))]
    row_spec = pl.BlockSpec((tm, n), lambda i: (i, 0))
    if final:
        out_shape = (jax.ShapeDtypeStruct((m, n), F32),)
        out_specs = (row_spec,)
    else:
        out_shape = (jax.ShapeDtypeStruct((m, n), F32), jax.ShapeDtypeStruct((m, n), BF16))
        out_specs = (row_spec, row_spec)
    outs = pl.pallas_call(
        kern,
        grid=(m // tm,),
        in_specs=[
            pl.BlockSpec((tm, k), lambda i: (i, 0)),
            w_spec,
            row_spec,
            pl.BlockSpec((1, n), lambda i: (0, 0)),
        ],
        out_specs=out_specs,
        out_shape=out_shape,
        scratch_shapes=scratch,
        compiler_params=_params("arbitrary"),
        name=name,
    )(a, w, res, g.reshape(1, n))
    return outs[0] if final else outs


def _attn_kernel(rb_ref, bkt_ref, q_ref, k_ref, v_ref, side_in_ref, o_ref, side_out_ref,
                 kmean_ref, kaug_ref, vt_ref, bias_ref, qaug_ref, s0_ref, s1_ref, p1_ref, acc_ref,
                 *, nb, qb, n_sel):
    bs = MOBA_BLOCK
    hd = HEAD_DIM
    qw = qb * bs
    h = pl.program_id(0)
    i0 = pl.program_id(1) * qb
    neg_inf = jnp.float32(-jnp.inf)
    side_out_ref[...] = side_in_ref[...].astype(BF16)

    @pl.when(pl.program_id(1) == 0)
    def _per_head_setup():
        bkt = bkt_ref[...]
        bias_row = jnp.zeros(bkt.shape, F32)
        for b in range(NUM_BUCKETS):
            bias_row = jnp.where(bkt == b, rb_ref[b, h] * LOG2E, bias_row)
        for d in range(nb):
            seg = jnp.broadcast_to(bias_row[:, d * bs:(d + 2) * bs], (bs, 2 * bs))
            tile = pltpu.roll(seg, 0, 1, stride=1, stride_axis=0)[:, bs:]
            if d == 0:
                kpos = lax.broadcasted_iota(jnp.int32, tile.shape, 0)
                qpos = lax.broadcasted_iota(jnp.int32, tile.shape, 1)
                tile = jnp.where(kpos <= qpos, tile, MASKED)
            bias_ref[d] = tile
        lane_blk = lax.broadcasted_iota(jnp.int32, (bs, V7X_LANES), 1)
        for j in range(nb):
            kj = k_ref[j * bs:(j + 1) * bs, :]
            kmean_ref[j:j + 1, :] = jnp.mean(kj.astype(F32), axis=0, keepdims=True)
            kaug_ref[j * bs:(j + 1) * bs, :] = jnp.concatenate(
                [kj, jnp.where(lane_blk == j, 1.0, 0.0).astype(BF16)], axis=1)
            ones_row = jnp.where(lax.broadcasted_iota(jnp.int32, (ATTN_SUM_ROWS, bs), 0) == 0, 1.0, 0.0)
            vt_ref[j] = jnp.concatenate([v_ref[j * bs:(j + 1) * bs, :].astype(F32).T, ones_row], axis=0).astype(BF16)

    qt = q_ref[...].astype(F32).T.astype(BF16)
    qaug_ref[:hd, :] = qt

    gate = jnp.dot(kmean_ref[...].astype(BF16), qt, preferred_element_type=F32)
    blk = lax.broadcasted_iota(jnp.int32, gate.shape, 0)
    cur = i0 + lax.broadcasted_iota(jnp.int32, gate.shape, 1) // bs
    gate = jnp.where(blk < cur, gate, neg_inf)
    madd = jnp.where(blk == cur, 0.0, MASKED).astype(F32)
    for _ in range(n_sel):
        top = jnp.max(gate, axis=0, keepdims=True)
        idx = jnp.min(jnp.where(gate == top, blk, nb), axis=0, keepdims=True)
        hit = blk == idx
        madd = jnp.where(jnp.logical_and(hit, top > neg_inf), 0.0, madd)
        gate = jnp.where(hit, neg_inf, gate)
    qaug_ref[hd:, :] = jnp.concatenate([madd, jnp.zeros((V7X_LANES - nb, qw), F32)], axis=0).astype(BF16)

    def scores(j, lo):
        kj = kaug_ref[pl.ds(pl.multiple_of(j * bs, bs), bs), :]
        s = jnp.dot(kj, qaug_ref[:, lo:], preferred_element_type=F32)
        bias = jnp.concatenate([bias_ref[jnp.maximum(i0 + a - j, 0)] for a in range(lo // bs, qb)], axis=1)
        return s + bias


    def own_block(u, m, s_cur_ref, s_next_ref, p_prev):
        lo, lo_prev = u * bs, max(u - 1, 0) * bs
        s = s_cur_ref[:, lo:]
        m_new = jnp.maximum(m[:, lo:], jnp.max(s, axis=0, keepdims=True))
        p = jnp.exp2(s - m_new).astype(BF16)
        if p_prev is not None:
            pv_prev = jnp.dot(vt_ref[i0 + u - 1], p_prev, preferred_element_type=F32)
            acc_ref[:, lo_prev:lo] = acc_ref[:, lo_prev:lo] + pv_prev[:, :lo - lo_prev]
            acc_ref[:, lo:] = jnp.exp2(m[:, lo:] - m_new) * (acc_ref[:, lo:] + pv_prev[:, lo - lo_prev:])
        if u + 1 < qb:
            s_next_ref[:, lo + bs:] = scores(i0 + u + 1, lo + bs)
        else:
            s_next_ref[...] = scores(0, 0)
        return (m_new if lo == 0 else jnp.concatenate([m[:, :lo], m_new], axis=1)), p

    def past_block(j, m, s_cur_ref, s_next_ref, p_prev):
        pv_prev = jnp.dot(vt_ref[jnp.maximum(j - 1, 0)], p_prev, preferred_element_type=F32)
        s = s_cur_ref[...]
        m_new = jnp.maximum(m, jnp.max(s, axis=0, keepdims=True))
        p = jnp.exp2(s - m_new).astype(BF16)
        acc_ref[...] = jnp.exp2(m - m_new) * (acc_ref[...] + pv_prev)
        s_next_ref[...] = scores(jnp.minimum(j + 1, i0 - 1), 0)
        return m_new, p

    def past_group(g, m):
        p = p1_ref[...]
        for u in range(0, qb, 2):
            m, p = past_block(qb * g + u, m, s0_ref, s1_ref, p)
            m, p = past_block(qb * g + u + 1, m, s1_ref, s0_ref, p)
        p1_ref[...] = p
        return m

    s0_ref[...] = scores(i0, 0)
    p1_ref[...] = jnp.zeros(p1_ref.shape, BF16)
    acc_ref[...] = jnp.zeros(acc_ref.shape, F32)
    m = jnp.full((1, qw), MASKED, F32)
    p = None
    for u in range(0, qb, 2):
        m, p = own_block(u, m, s0_ref, s1_ref, p)
        m, p = own_block(u + 1, m, s1_ref, s0_ref, p)
    lo = (qb - 1) * bs
    acc_ref[:, lo:] = acc_ref[:, lo:] + jnp.dot(vt_ref[i0 + qb - 1], p, preferred_element_type=F32)
    lax.fori_loop(0, pl.program_id(1), past_group, m)
    acc = acc_ref[...] + jnp.dot(vt_ref[jnp.maximum(i0 - 1, 0)], p1_ref[...], preferred_element_type=F32)
    o_ref[...] = (acc[:hd] / acc[hd:hd + 1]).T.astype(o_ref.dtype)


def _t5_bucket(dist):
    n = jnp.maximum(dist, 0)
    max_exact = NUM_BUCKETS // 2
    nf = jnp.maximum(n, max_exact).astype(F32)
    large = max_exact + (jnp.log(nf / max_exact) / math.log(MAX_DISTANCE / max_exact)
                         * (NUM_BUCKETS - max_exact)).astype(jnp.int32)
    large = jnp.minimum(large, NUM_BUCKETS - 1)
    return jnp.where(n < max_exact, n, large)


def _moba_attention(qkv, rel_bias, side_w, side_layer):
    s, three_d = qkv.shape
    d = three_d // 3
    nh = d // HEAD_DIM
    bs = MOBA_BLOCK
    assert s % bs == 0 and d % HEAD_DIM == 0
    nb = s // bs
    n_sel = min(MOBA_TOPK, nb - 1)
    qb = _tile(nb, ATTN_QUERY_BLOCKS, 2)
    assert nb <= V7X_LANES, "the block one-hot code occupies one lane tile"
    bkt = _t5_bucket(jnp.arange(-bs, s, dtype=jnp.int32)).reshape(1, s + bs)
    kern = functools.partial(_attn_kernel, nb=nb, qb=qb, n_sel=n_sel)
    n_ib = nb // qb
    side_in, side_out, side_shape = _side_cast_specs(side_w, side_layer, nh * n_ib, lambda h, i: h * n_ib + i)
    return pl.pallas_call(
        kern,
        grid=(nh, n_ib),
        in_specs=[
            pl.BlockSpec(memory_space=pltpu.SMEM),
            pl.BlockSpec((1, s + bs), lambda h, i: (0, 0)),
            pl.BlockSpec((qb * bs, HEAD_DIM), lambda h, i: (i, h)),
            pl.BlockSpec((s, HEAD_DIM), lambda h, i: (0, nh + h)),
            pl.BlockSpec((s, HEAD_DIM), lambda h, i: (0, 2 * nh + h)),
            side_in,
        ],
        out_specs=(pl.BlockSpec((qb * bs, HEAD_DIM), lambda h, i: (i, h)), side_out),
        out_shape=(jax.ShapeDtypeStruct((s, d), BF16), side_shape),
        scratch_shapes=[
            pltpu.VMEM((nb, HEAD_DIM), F32),
            pltpu.VMEM((s, HEAD_DIM + V7X_LANES), BF16),
            pltpu.VMEM((nb, HEAD_DIM + ATTN_SUM_ROWS, bs), BF16),
            pltpu.VMEM((nb, bs, bs), F32),
            pltpu.VMEM((HEAD_DIM + V7X_LANES, qb * bs), BF16),
            pltpu.VMEM((bs, qb * bs), F32),
            pltpu.VMEM((bs, qb * bs), F32),
            pltpu.VMEM((bs, qb * bs), BF16),
            pltpu.VMEM((HEAD_DIM + ATTN_SUM_ROWS, qb * bs), F32),
        ],
        compiler_params=_params("arbitrary", "arbitrary"),
        name="moba_attention",
    )(rel_bias, bkt, qkv, qkv, qkv, side_w)


def _sigmoid(x):
    return 0.5 * jnp.tanh(0.5 * x) + 0.5


def _rglru_kernel(xz_ref, cw_ref, cb_ref, wrg_ref, brg_ref, wig_ref, big_ref, lam_ref, side_in_ref,
                  y_ref, side_out_ref, xbuf_ref, xc_ref, a_ref, b_ref, h_ref, *, tt, c):
    pad = V7X_SUBLANES
    side_out_ref[...] = side_in_ref[...].astype(BF16)

    @pl.when(pl.program_id(0) == 0)
    def _():
        xbuf_ref[0:pad, :] = jnp.zeros((pad, c), F32)
        h_ref[...] = jnp.zeros((1, c), F32)

    xbuf_ref[pad:pad + tt, :] = xz_ref[:, :c]
    xc = cb_ref[...]
    for j in range(CONV_WIDTH):
        lag = CONV_WIDTH - 1 - j
        xc = xc + xbuf_ref[pad - lag:pad - lag + tt, :] * cw_ref[j:j + 1, :]
    xc_ref[...] = xc
    xbuf_ref[0:pad, :] = xbuf_ref[tt:tt + pad, :]

    for n in range(c // GATE_BLOCK):
        sl = slice(n * GATE_BLOCK, (n + 1) * GATE_BLOCK)
        xcn = xc_ref[:, sl]
        xb = xcn.astype(BF16)
        t_r = jnp.tanh(0.5 * (jnp.dot(xb, wrg_ref[n], preferred_element_type=F32) + brg_ref[:, sl]))
        ig = _sigmoid(jnp.dot(xb, wig_ref[n], preferred_element_type=F32) + big_ref[:, sl])
        half = (0.5 * LRU_C) * jax.nn.log_sigmoid(lam_ref[:, sl])
        log_a = t_r * half + half
        a = jnp.exp(log_a)
        a_ref[:, sl] = a
        w = -jnp.tanh(log_a) * (1.0 + a * a)
        b_ref[:, sl] = jnp.where(w > 0.0, w * lax.rsqrt(w), 0.0) * (ig * xcn)

    def step(t, hprev):
        hnew = a_ref[pl.ds(t, 1), :] * hprev + b_ref[pl.ds(t, 1), :]
        b_ref[pl.ds(t, 1), :] = hnew
        return hnew

    h_ref[...] = lax.fori_loop(0, tt, step, h_ref[...], unroll=8)
    g = xz_ref[:, c:]
    inner = g * (GELU_C1 + (GELU_C1 * GELU_C3) * (g * g))
    y_ref[...] = ((b_ref[...] * (0.5 * g)) * (1.0 + jnp.tanh(inner))).astype(y_ref.dtype)


def _rglru(xz, conv_w, conv_b, w_rg, b_rg, w_ig, b_ig, lam, side_w, side_layer):
    s, two_c = xz.shape
    c = two_c // 2
    assert c % GATE_BLOCK == 0
    ngb = c // GATE_BLOCK
    tt = _tile(s, 256, V7X_SUBLANES)
    kern = functools.partial(_rglru_kernel, tt=tt, c=c)
    vec = pl.BlockSpec((1, c), lambda t: (0, 0))
    gate_w = pl.BlockSpec((ngb, GATE_BLOCK, GATE_BLOCK), lambda t: (0, 0, 0))
    side_in, side_out, side_shape = _side_cast_specs(side_w, side_layer, s // tt, lambda t: t)
    return pl.pallas_call(
        kern,
        grid=(s // tt,),
        in_specs=[
            pl.BlockSpec((tt, two_c), lambda t: (t, 0)),
            pl.BlockSpec((CONV_WIDTH, c), lambda t: (0, 0)),
            vec, gate_w, vec, gate_w, vec, vec, side_in,
        ],
        out_specs=(pl.BlockSpec((tt, c), lambda t: (t, 0)), side_out),
        out_shape=(jax.ShapeDtypeStruct((s, c), BF16), side_shape),
        scratch_shapes=[
            pltpu.VMEM((tt + 2 * V7X_SUBLANES, c), F32),
            pltpu.VMEM((tt, c), F32),
            pltpu.VMEM((tt, c), F32),
            pltpu.VMEM((tt, c), F32),
            pltpu.VMEM((1, c), F32),
        ],
        compiler_params=_params("arbitrary"),
        name="rglru_core",
    )(xz, conv_w, conv_b.reshape(1, c), w_rg.astype(BF16), b_rg.reshape(1, c),
      w_ig.astype(BF16), b_ig.reshape(1, c), lam.reshape(1, c), side_w)


def kernel(x, rel_bias, mix_norm, mlp_norm, final_norm, attn_w_qkv, attn_w_o, rec_w_in, rec_conv_w,
           rec_conv_b, rec_w_rg, rec_b_rg, rec_w_ig, rec_b_ig, rec_lambda, rec_w_out, mlp_w1, mlp_w2):
    b, s, d = x.shape
    assert b == 1, "the trunk kernels handle one sequence"
    depth = mix_norm.shape[0]
    xr = x.reshape(s, d)
    xn = _rmsnorm(xr, mix_norm[0])
    out = None
    for layer in range(depth):
        j = layer // 2
        if layer % 2 == 0:
            qkv = _proj(xn, attn_w_qkv, j, BF16, "attn_qkv_proj",
                        epilogue="scale_leading", scaled_cols=d, scale=HEAD_DIM ** -0.5 * LOG2E)
            mixed, w2 = _moba_attention(qkv, rel_bias, mlp_w2, layer)
            w_out = attn_w_o
        else:
            xz = _proj(xn, rec_w_in, j, F32, "rec_in_proj")
            mixed, w2 = _rglru(xz, rec_conv_w[j], rec_conv_b[j], rec_w_rg[j], rec_b_rg[j],
                               rec_w_ig[j], rec_b_ig[j], rec_lambda[j], mlp_w2, layer)
            w_out = rec_w_out
        xr, xn = _resid_proj(mixed, w_out, j, xr, mlp_norm[layer], "mixer_out_proj")
        hidden = _proj(xn, mlp_w1, layer, BF16, "mlp_up_proj", epilogue="relu2")
        if layer + 1 < depth:
            xr, xn = _resid_proj(hidden, w2[None], 0, xr, mix_norm[layer + 1], "mlp_down_proj")
        else:
            out = _resid_proj(hidden, w2[None], 0, xr, final_norm, "mlp_down_proj_final", final=True)
    return out.reshape(b, s, d)
```
